```python
import jax, jax.numpy as jnp
from jax import lax
import numpy as np

D_MODEL = 1024
BATCH = 8
SEQ = 4096
DEPTH = 2

GMLP_WIDTH = D_MODEL
GMLP_GROUPS = 8
GMLP_GROUP_DIM = GMLP_WIDTH // GMLP_GROUPS
CHUNK = 128
LRU_WIDTH = D_MODEL
LRU_HEADS = 8
LRU_HEAD_DIM = LRU_WIDTH // LRU_HEADS
CONV_WIDTH = 4
CONV_LEFT = 1
LRU_C = 8.0
N_DIRS = 2
D_FF = -(-8 * D_MODEL // (3 * 256)) * 256
N_IN = 2 * GMLP_WIDTH + 2 * LRU_WIDTH + 2 * D_MODEL
EPS = 1e-6

kernel_name = "hybrid_gmlp_rglru_encoder"


def _rmsnorm(x, g):
    x32 = x.astype(jnp.float32)
    y = x32 * lax.rsqrt(jnp.mean(x32 * x32, axis=-1, keepdims=True) + EPS)
    return (y * g.astype(jnp.float32)).astype(x.dtype)


def _layernorm(x, g, b):
    x32 = x.astype(jnp.float32)
    mu = jnp.mean(x32, axis=-1, keepdims=True)
    xc = x32 - mu
    y = xc * lax.rsqrt(jnp.mean(xc * xc, axis=-1, keepdims=True) + EPS)
    return (y * g.astype(jnp.float32) + b.astype(jnp.float32)).astype(x.dtype)


def _blockdiag(x, w, b):
    B, S, _ = x.shape
    xh = x.reshape(B, S, LRU_HEADS, LRU_HEAD_DIM)
    y = jnp.einsum('bshc,hcd->bshd', xh, w.astype(x.dtype))
    return y.reshape(B, S, LRU_WIDTH) + b.astype(x.dtype)


def _lin_combine(left, right):
    a1, b1 = left
    a2, b2 = right
    return a1 * a2, a2 * b1 + b2


def _rglru_scan(x32, w_r, b_r, w_i, b_i, lam, reverse):
    r = jax.nn.sigmoid(_blockdiag(x32, w_r, b_r))
    i = jax.nn.sigmoid(_blockdiag(x32, w_i, b_i))
    log_a = -LRU_C * r * jax.nn.softplus(-lam.astype(jnp.float32))
    a = jnp.exp(log_a)
    mult = jnp.sqrt(jnp.maximum(-jnp.expm1(2.0 * log_a), 0.0))
    bt = mult * (i * x32)
    _, h = lax.associative_scan(_lin_combine, (a, bt), axis=1, reverse=reverse)
    return h


def _gmlp_branch(zu, zv, ln_g, ln_b, w_s, b_s):
    B, S, _ = zu.shape
    u = jax.nn.gelu(zu)
    v = _layernorm(jax.nn.gelu(zv), ln_g, ln_b)
    vc = v.reshape(B, S // CHUNK, CHUNK, GMLP_GROUPS, GMLP_GROUP_DIM)
    mixed = jnp.einsum('gpq,bnqgc->bnpgc', w_s.astype(v.dtype), vc)
    mixed = mixed + b_s.T.astype(v.dtype)[None, None, :, :, None]
    return u * mixed.reshape(B, S, GMLP_WIDTH)


def _rglru_branch(zx, zg, conv_w, conv_b, w_r, b_r, w_i, b_i, lam):
    S = zx.shape[1]
    xp = jnp.pad(zx, ((0, 0), (CONV_LEFT, CONV_WIDTH - 1 - CONV_LEFT), (0, 0)))
    xc = conv_b.astype(zx.dtype) + sum(xp[:, k:k + S] * conv_w[k].astype(zx.dtype) for k in range(CONV_WIDTH))
    x32 = xc.astype(jnp.float32)
    h = (_rglru_scan(x32, w_r[0], b_r[0], w_i[0], b_i[0], lam[0], False)
         + _rglru_scan(x32, w_r[1], b_r[1], w_i[1], b_i[1], lam[1], True))
    return h.astype(zx.dtype) * jax.nn.gelu(zg)


def setup_inputs(seed: int = 0) -> dict:
    key = jax.random.key(seed)
    ks = jax.random.split(key, 24)
    f32 = jnp.float32
    nrm = lambda k, shape, s: jax.random.normal(k, shape, f32) * s
    L = DEPTH
    x = jax.random.normal(ks[0], (BATCH, SEQ, D_MODEL), f32)
    norm1_g = 1.0 + nrm(ks[1], (L, D_MODEL), 0.05)
    w_in = nrm(ks[2], (L, D_MODEL, N_IN), D_MODEL ** -0.5)
    gmlp_ln_g = 1.0 + nrm(ks[3], (L, GMLP_WIDTH), 0.05)
    gmlp_ln_b = nrm(ks[4], (L, GMLP_WIDTH), 0.05)
    gmlp_w_s = nrm(ks[5], (L, GMLP_GROUPS, CHUNK, CHUNK), CHUNK ** -0.5)
    gmlp_b_s = 1.0 + nrm(ks[6], (L, GMLP_GROUPS, CHUNK), 0.1)
    conv_w = nrm(ks[7], (L, CONV_WIDTH, LRU_WIDTH), CONV_WIDTH ** -0.5)
    conv_b = nrm(ks[8], (L, LRU_WIDTH), 0.02)
    lru_w_r = nrm(ks[9], (L, N_DIRS, LRU_HEADS, LRU_HEAD_DIM, LRU_HEAD_DIM), LRU_HEAD_DIM ** -0.5)
    lru_b_r = nrm(ks[10], (L, N_DIRS, LRU_WIDTH), 0.02)
    lru_w_i = nrm(ks[11], (L, N_DIRS, LRU_HEADS, LRU_HEAD_DIM, LRU_HEAD_DIM), LRU_HEAD_DIM ** -0.5)
    lru_b_i = nrm(ks[12], (L, N_DIRS, LRU_WIDTH), 0.02)
    a_c = jax.random.uniform(ks[13], (L, N_DIRS, LRU_WIDTH), f32, 0.9, 0.999)
    a0 = a_c ** (1.0 / LRU_C)
    lru_lambda = jnp.log(a0) - jnp.log1p(-a0)
    w_out = nrm(ks[14], (L, D_MODEL, D_MODEL), D_MODEL ** -0.5)
    norm2_g = 1.0 + nrm(ks[15], (L, D_MODEL), 0.05)
    w_ffn_in = nrm(ks[16], (L, D_MODEL, 2 * D_FF), D_MODEL ** -0.5)
    w_ffn_out = nrm(ks[17], (L, D_FF, D_MODEL), D_FF ** -0.5)
    final_g = 1.0 + nrm(ks[18], (D_MODEL,), 0.05)
    return {"x": x, "norm1_g": norm1_g, "w_in": w_in, "gmlp_ln_g": gmlp_ln_g,
            "gmlp_ln_b": gmlp_ln_b, "gmlp_w_s": gmlp_w_s, "gmlp_b_s": gmlp_b_s,
            "conv_w": conv_w, "conv_b": conv_b, "lru_w_r": lru_w_r, "lru_b_r": lru_b_r,
            "lru_w_i": lru_w_i, "lru_b_i": lru_b_i, "lru_lambda": lru_lambda,
            "w_out": w_out, "norm2_g": norm2_g, "w_ffn_in": w_ffn_in,
            "w_ffn_out": w_ffn_out, "final_g": final_g}


def reference(x, norm1_g, w_in, gmlp_ln_g, gmlp_ln_b, gmlp_w_s, gmlp_b_s, conv_w, conv_b,
              lru_w_r, lru_b_r, lru_w_i, lru_b_i, lru_lambda, w_out, norm2_g, w_ffn_in,
              w_ffn_out, final_g):
    c0 = GMLP_WIDTH
    c1 = 2 * GMLP_WIDTH
    c2 = c1 + LRU_WIDTH
    c3 = c2 + LRU_WIDTH
    c4 = c3 + D_MODEL
    for l in range(DEPTH):
        h = _rmsnorm(x, norm1_g[l])
        z = h @ w_in[l].astype(h.dtype)
        y_a = _gmlp_branch(z[..., :c0], z[..., c0:c1], gmlp_ln_g[l], gmlp_ln_b[l],
                           gmlp_w_s[l], gmlp_b_s[l])
        y_b = _rglru_branch(z[..., c1:c2], z[..., c2:c3], conv_w[l], conv_b[l],
                            lru_w_r[l], lru_b_r[l], lru_w_i[l], lru_b_i[l], lru_lambda[l])
        merged = jax.nn.sigmoid(z[..., c3:c4]) * y_a + jax.nn.sigmoid(z[..., c4:]) * y_b
        x = x + merged @ w_out[l].astype(merged.dtype)
        h = _rmsnorm(x, norm2_g[l])
        gu = h @ w_ffn_in[l].astype(h.dtype)
        ff = jax.nn.silu(gu[..., :D_FF]) * gu[..., D_FF:]
        x = x + ff @ w_ffn_out[l].astype(ff.dtype)
    return _rmsnorm(x, final_g)
```

```python
import functools

import jax
import jax.numpy as jnp
from jax import lax
from jax.experimental import pallas as pl
from jax.experimental.pallas import tpu as pltpu

D_MODEL = 1024
CHUNK = 128
GROUPS = 8
GROUP_DIM = D_MODEL // GROUPS
HEADS = 8
HEAD_DIM = D_MODEL // HEADS
CONV_WIDTH = 4
CONV_LEFT = 1
LRU_C = 8.0
N_DIRS = 2
D_FF = 2816
EPS = 1e-6

SUBLANES = 8
LANES = 128
ROW_TILE = 512
FF_COLS = 256
SCAN_ROWS = 256
VMEM_LIMIT = 56 * 1024 * 1024

F32 = jnp.float32
BF16 = jnp.bfloat16


def _gelu(x):
    return x * (0.5 * (1.0 + jnp.tanh(0.7978845608028654 * (x + 0.044715 * (x * x * x)))))


def _sigmoid(x):
    return 1.0 / (1.0 + jnp.exp(-x))


def _rms_scale(x):
    return x * lax.rsqrt(jnp.mean(x * x, axis=-1, keepdims=True) + EPS)


def _const_spec(shape):
    return pl.BlockSpec(shape, lambda *_: (0,) * len(shape), pipeline_mode=pl.Buffered(1))


def _proj_in_kernel(x_ref, g1_ref, w_ref, lng_ref, lnb_ref, ws_ref, bs_ref,
                    ya_ref, zx_ref, gb_ref, vn_scr):
    hb = (_rms_scale(x_ref[...]) * g1_ref[...]).astype(BF16)

    def proj(col):
        return jnp.dot(hb, w_ref[:, col * D_MODEL:(col + 1) * D_MODEL],
                       preferred_element_type=F32)

    zx_ref[...] = proj(2)
    gb_ref[...] = _sigmoid(proj(5)) * _gelu(proj(3))

    gv = _gelu(proj(1))
    dv = gv - jnp.mean(gv, axis=-1, keepdims=True)
    vn = dv * lax.rsqrt(jnp.mean(dv * dv, axis=-1, keepdims=True) + EPS)
    vn_scr[...] = (vn * lng_ref[...] + lnb_ref[...]).astype(BF16)

    ya_ref[...] = _sigmoid(proj(4)) * _gelu(proj(0))
    for g in range(GROUPS):
        cols = slice(g * GROUP_DIM, (g + 1) * GROUP_DIM)
        for c in range(ROW_TILE // CHUNK):
            rows = slice(c * CHUNK, (c + 1) * CHUNK)
            mixed = jnp.dot(ws_ref[g], vn_scr[rows, cols], preferred_element_type=F32)
            ya_ref[rows, cols] = ya_ref[rows, cols] * (mixed + bs_ref[g])


def _proj_in(x, g1, w_in, ln_g, ln_b, w_s, b_s):
    t = x.shape[0]
    row_spec = pl.BlockSpec((ROW_TILE, D_MODEL), lambda i: (i, 0))
    out = jax.ShapeDtypeStruct((t, D_MODEL), F32)
    return pl.pallas_call(
        _proj_in_kernel,
        grid=(t // ROW_TILE,),
        in_specs=[
            row_spec,
            _const_spec((1, D_MODEL)),
            _const_spec(w_in.shape),
            _const_spec((1, D_MODEL)),
            _const_spec((1, D_MODEL)),
            _const_spec(w_s.shape),
            _const_spec(b_s.shape),
        ],
        out_specs=[row_spec, row_spec, row_spec],
        out_shape=[out, out, out],
        scratch_shapes=[pltpu.VMEM((ROW_TILE, D_MODEL), BF16)],
        compiler_params=pltpu.CompilerParams(
            dimension_semantics=("arbitrary",), vmem_limit_bytes=VMEM_LIMIT),
        name="proj_in",
    )(x, g1, w_in, ln_g, ln_b, w_s, b_s)


def _tile_scan(a, b, reverse):
    rows = a.shape[0]
    a = a.reshape(rows // SUBLANES, SUBLANES, LANES)
    b = b.reshape(rows // SUBLANES, SUBLANES, LANES)
    pos = lax.broadcasted_iota(jnp.int32, a.shape, 1)
    step = 1
    while step < SUBLANES:
        if reverse:
            shift, valid = SUBLANES - step, pos < SUBLANES - step
        else:
            shift, valid = step, pos >= step
        a_nb = jnp.where(valid, pltpu.roll(a, shift, 1), 1.0)
        b_nb = jnp.where(valid, pltpu.roll(b, shift, 1), 0.0)
        b = a * b_nb + b
        a = a * a_nb
        step *= 2
    return a.reshape(rows, LANES), b.reshape(rows, LANES)


def _carry_chain(a_cum, b_loc, carry, reverse):
    n = a_cum.shape[0] // SUBLANES
    order = range(n - 1, -1, -1) if reverse else range(n)
    edge = 0 if reverse else SUBLANES - 1
    tiles = [None] * n
    for j in order:
        rows = slice(j * SUBLANES, (j + 1) * SUBLANES)
        h = a_cum[rows] * carry + b_loc[rows]
        tiles[j] = h
        carry = jnp.broadcast_to(h[edge:edge + 1], (SUBLANES, LANES))
    return jnp.concatenate(tiles, axis=0), carry


def _lru_kernel(zx_ref, gb_ref, ya_ref, cw_ref, cb_ref, wg_ref, bg_ref, lam_ref,
                out_ref, xpad, hf_scr, a1_scr, b1_scr):
    seq = zx_ref.shape[1]
    n_blocks = seq // SCAN_ROWS
    halo = SUBLANES

    zeros = jnp.zeros((halo, LANES), F32)
    xpad[0:halo, :] = zeros
    xpad[halo + seq:2 * halo + seq, :] = zeros
    xpad[halo:halo + seq, :] = zx_ref[0]

    neg_lam = -lam_ref[...]
    softplus = jnp.maximum(neg_lam, 0.0) + jnp.log1p(jnp.exp(-jnp.abs(neg_lam)))

    def decay_and_input(r, i, xc, d):
        log_a = (-LRU_C * r) * softplus[d:d + 1]
        a = jnp.exp(log_a)
        mult = jnp.sqrt(jnp.maximum(-jnp.tanh(log_a) * (a * a + 1.0), 0.0))
        return a, mult * (i * xc)

    def forward_block(c, carry):
        t0 = pl.multiple_of(c * SCAN_ROWS, SCAN_ROWS)
        win = xpad[pl.ds(t0, SCAN_ROWS + 2 * halo), :]
        xc = cb_ref[...]
        for k in range(CONV_WIDTH):
            lo = halo + k - CONV_LEFT
            xc = xc + win[lo:lo + SCAN_ROWS] * cw_ref[k:k + 1, :]
        gates = jnp.dot(xc.astype(BF16), wg_ref[0], preferred_element_type=F32) + bg_ref[0]
        gates = _sigmoid(gates)
        a0, b0 = decay_and_input(gates[:, 0:LANES], gates[:, LANES:2 * LANES], xc, 0)
        a1, b1 = decay_and_input(gates[:, 2 * LANES:3 * LANES], gates[:, 3 * LANES:], xc, 1)
        a0, b0 = _tile_scan(a0, b0, reverse=False)
        h, carry = _carry_chain(a0, b0, carry, reverse=False)
        hf_scr[pl.ds(t0, SCAN_ROWS), :] = h
        a1, b1 = _tile_scan(a1, b1, reverse=True)
        a1_scr[pl.ds(t0, SCAN_ROWS), :] = a1
        b1_scr[pl.ds(t0, SCAN_ROWS), :] = b1
        return carry

    def backward_block(c, carry):
        t0 = pl.multiple_of((n_blocks - 1 - c) * SCAN_ROWS, SCAN_ROWS)
        rows = pl.ds(t0, SCAN_ROWS)
        hb, carry = _carry_chain(a1_scr[rows, :], b1_scr[rows, :], carry, reverse=True)
        merged = ya_ref[0, rows, :] + gb_ref[0, rows, :] * (hf_scr[rows, :] + hb)
        out_ref[0, rows, :] = merged.astype(out_ref.dtype)
        return carry

    init = jnp.zeros((SUBLANES, LANES), F32)
    lax.fori_loop(0, n_blocks, forward_block, init)
    lax.fori_loop(0, n_blocks, backward_block, init)


def _lru(zx, gb, ya, conv_w, conv_b, w_gates, b_gates, lam):
    bsz, seq, _ = zx.shape
    slab = pl.BlockSpec((1, seq, HEAD_DIM), lambda b, h: (b, 0, h))
    per_head = lambda rows: pl.BlockSpec((rows, HEAD_DIM), lambda b, h: (0, h))
    return pl.pallas_call(
        _lru_kernel,
        grid=(bsz, HEADS),
        in_specs=[
            slab, slab, slab,
            per_head(CONV_WIDTH),
            per_head(1),
            pl.BlockSpec((1, HEAD_DIM, 4 * HEAD_DIM), lambda b, h: (h, 0, 0)),
            pl.BlockSpec((1, 1, 4 * HEAD_DIM), lambda b, h: (h, 0, 0)),
            per_head(N_DIRS),
        ],
        out_specs=slab,
        out_shape=jax.ShapeDtypeStruct(zx.shape, BF16),
        scratch_shapes=[
            pltpu.VMEM((seq + 2 * SUBLANES, HEAD_DIM), F32),
            pltpu.VMEM((seq, HEAD_DIM), F32),
            pltpu.VMEM((seq, HEAD_DIM), F32),
            pltpu.VMEM((seq, HEAD_DIM), F32),
        ],
        compiler_params=pltpu.CompilerParams(
            dimension_semantics=("arbitrary", "arbitrary"), vmem_limit_bytes=VMEM_LIMIT),
        name="rglru_merge",
    )(zx, gb, ya, conv_w, conv_b, w_gates, b_gates, lam)


def _ffn_kernel(x_ref, m_ref, wo_ref, g2_ref, wi_ref, wf_ref, fg_ref, o_ref, ff_scr, *, final):
    x1 = x_ref[...] + jnp.dot(m_ref[...], wo_ref[...], preferred_element_type=F32)
    o_ref[...] = x1
    hb = (_rms_scale(x1) * g2_ref[...]).astype(BF16)
    for j in range(D_FF // FF_COLS):
        cols = slice(j * FF_COLS, (j + 1) * FF_COLS)
        gate = jnp.dot(hb, wi_ref[:, cols], preferred_element_type=F32)
        up = jnp.dot(hb, wi_ref[:, D_FF + j * FF_COLS:D_FF + (j + 1) * FF_COLS],
                     preferred_element_type=F32)
        ff_scr[:, cols] = (gate * _sigmoid(gate) * up).astype(BF16)
    x2 = o_ref[...] + jnp.dot(ff_scr[...], wf_ref[...], preferred_element_type=F32)
    if final:
        x2 = _rms_scale(x2) * fg_ref[...]
    o_ref[...] = x2


def _ffn(x, merged, w_out, g2, w_ffn_in, w_ffn_out, final_g, final):
    t = x.shape[0]
    row_spec = pl.BlockSpec((ROW_TILE, D_MODEL), lambda i: (i, 0))
    return pl.pallas_call(
        functools.partial(_ffn_kernel, final=final),
        grid=(t // ROW_TILE,),
        in_specs=[
            row_spec, row_spec,
            _const_spec(w_out.shape),
            _const_spec((1, D_MODEL)),
            _const_spec(w_ffn_in.shape),
            _const_spec(w_ffn_out.shape),
            _const_spec((1, D_MODEL)),
        ],
        out_specs=row_spec,
        out_shape=jax.ShapeDtypeStruct((t, D_MODEL), F32),
        scratch_shapes=[pltpu.VMEM((ROW_TILE, D_FF), BF16)],
        compiler_params=pltpu.CompilerParams(
            dimension_semantics=("arbitrary",), vmem_limit_bytes=VMEM_LIMIT),
        name="out_proj_ffn",
    )(x, merged, w_out, g2, w_ffn_in, w_ffn_out, final_g)


def kernel(x, norm1_g, w_in, gmlp_ln_g, gmlp_ln_b, gmlp_w_s, gmlp_b_s, conv_w, conv_b,
           lru_w_r, lru_b_r, lru_w_i, lru_b_i, lru_lambda, w_out, norm2_g, w_ffn_in,
           w_ffn_out, final_g):
    bsz, seq, d = x.shape
    depth = w_in.shape[0]
    assert d == D_MODEL and seq % SCAN_ROWS == 0 and (bsz * seq) % ROW_TILE == 0
    assert ROW_TILE % CHUNK == 0 and seq % ROW_TILE == 0
    xf = x.reshape(bsz * seq, d)
    row = lambda v: v.reshape(1, -1)
    for l in range(depth):
        b_s = jnp.broadcast_to(gmlp_b_s[l][:, :, None], (GROUPS, CHUNK, GROUP_DIM))
        ya, zx, gb = _proj_in(xf, row(norm1_g[l]), w_in[l].astype(BF16), row(gmlp_ln_g[l]),
                              row(gmlp_ln_b[l]), gmlp_w_s[l].astype(BF16), b_s)
        w_gates = jnp.concatenate(
            [lru_w_r[l, 0], lru_w_i[l, 0], lru_w_r[l, 1], lru_w_i[l, 1]], axis=-1).astype(BF16)
        b_gates = jnp.stack(
            [lru_b_r[l, 0], lru_b_i[l, 0], lru_b_r[l, 1], lru_b_i[l, 1]], axis=0
        ).reshape(4, HEADS, HEAD_DIM).transpose(1, 0, 2).reshape(HEADS, 1, 4 * HEAD_DIM)
        shape3 = (bsz, seq, d)
        merged = _lru(zx.reshape(shape3), gb.reshape(shape3), ya.reshape(shape3), conv_w[l],
                      row(conv_b[l]), w_gates, b_gates, lru_lambda[l])
        xf = _ffn(xf, merged.reshape(bsz * seq, d), w_out[l].astype(BF16), row(norm2_g[l]),
                  w_ffn_in[l].astype(BF16), w_ffn_out[l].astype(BF16), row(final_g),
                  final=(l == depth - 1))
    return xf.reshape(bsz, seq, d)
```

```python
import functools

import jax
import jax.numpy as jnp
from jax import lax
from jax.experimental import pallas as pl
from jax.experimental.pallas import tpu as pltpu

D_MODEL = 1024
CHUNK = 128
GROUPS = 8
GROUP_DIM = D_MODEL // GROUPS
HEADS = 8
HEAD_DIM = D_MODEL // HEADS
CONV_WIDTH = 4
CONV_LEFT = 1
LRU_C = 8.0
N_DIRS = 2
D_FF = 2816
EPS = 1e-6

SUBLANES = 8
LANES = 128
ROW_TILE = 512
FF_COLS = 256
N_SEG = SUBLANES
T_BLOCK = 128
MERGE_ROWS = 256
LOG2_E = 1.4426950408889634
VMEM_LIMIT = 56 * 1024 * 1024

F32 = jnp.float32
BF16 = jnp.bfloat16


def _gelu(x):
    return x * (0.5 * (1.0 + jnp.tanh(0.7978845608028654 * (x + 0.044715 * (x * x * x)))))


def _sigmoid(x):
    return 1.0 / (1.0 + jnp.exp(-x))


def _rms_scale(x):
    return x * lax.rsqrt(jnp.mean(x * x, axis=-1, keepdims=True) + EPS)


def _const_spec(shape):
    return pl.BlockSpec(shape, lambda *_: (0,) * len(shape), pipeline_mode=pl.Buffered(1))


def _proj_in_kernel(x_ref, g1_ref, w_ref, lng_ref, lnb_ref, ws_ref, bs_ref,
                    ya_ref, zx_ref, gb_ref, vn_scr):
    hb = (_rms_scale(x_ref[...]) * g1_ref[...]).astype(BF16)

    def proj(col):
        return jnp.dot(hb, w_ref[:, col * D_MODEL:(col + 1) * D_MODEL],
                       preferred_element_type=F32)

    zx_ref[...] = proj(2)
    gb_ref[...] = _sigmoid(proj(5)) * _gelu(proj(3))

    gv = _gelu(proj(1))
    dv = gv - jnp.mean(gv, axis=-1, keepdims=True)
    vn = dv * lax.rsqrt(jnp.mean(dv * dv, axis=-1, keepdims=True) + EPS)
    vn_scr[...] = (vn * lng_ref[...] + lnb_ref[...]).astype(BF16)

    ya_ref[...] = _sigmoid(proj(4)) * _gelu(proj(0))
    for g in range(GROUPS):
        cols = slice(g * GROUP_DIM, (g + 1) * GROUP_DIM)
        for c in range(ROW_TILE // CHUNK):
            rows = slice(c * CHUNK, (c + 1) * CHUNK)
            mixed = jnp.dot(ws_ref[g], vn_scr[rows, cols], preferred_element_type=F32)
            ya_ref[rows, cols] = ya_ref[rows, cols] * (mixed + bs_ref[g])


def _proj_in(x, g1, w_in, ln_g, ln_b, w_s, b_s):
    t = x.shape[0]
    row_spec = pl.BlockSpec((ROW_TILE, D_MODEL), lambda i: (i, 0))
    out = jax.ShapeDtypeStruct((t, D_MODEL), F32)
    return pl.pallas_call(
        _proj_in_kernel,
        grid=(t // ROW_TILE,),
        in_specs=[
            row_spec,
            _const_spec((1, D_MODEL)),
            _const_spec(w_in.shape),
            _const_spec((1, D_MODEL)),
            _const_spec((1, D_MODEL)),
            _const_spec(w_s.shape),
            _const_spec(b_s.shape),
        ],
        out_specs=[row_spec, row_spec, row_spec],
        out_shape=[out, out, out],
        scratch_shapes=[pltpu.VMEM((ROW_TILE, D_MODEL), BF16)],
        compiler_params=pltpu.CompilerParams(
            dimension_semantics=("arbitrary",), vmem_limit_bytes=VMEM_LIMIT),
        name="proj_in",
    )(x, g1, w_in, ln_g, ln_b, w_s, b_s)


def _tile_scan(a, b, reverse):
    pos = lax.broadcasted_iota(jnp.int32, a.shape, 0)
    step = 1
    while step < N_SEG:
        if reverse:
            shift, valid = N_SEG - step, pos < N_SEG - step
        else:
            shift, valid = step, pos >= step
        a_nb = jnp.where(valid, pltpu.roll(a, shift, 0), 1.0)
        b_nb = jnp.where(valid, pltpu.roll(b, shift, 0), 0.0)
        b = a * b_nb + b
        a = a * a_nb
        step *= 2
    return b


def _shift_segments(v, up):
    pos = lax.broadcasted_iota(jnp.int32, v.shape, 0)
    if up:
        return jnp.where(pos < N_SEG - 1, pltpu.roll(v, N_SEG - 1, 0), 0.0)
    return jnp.where(pos >= 1, pltpu.roll(v, 1, 0), 0.0)


def _lru_kernel(zx_ref, gb_ref, ya_ref, cw_ref, cb_ref, wg_ref, bg_ref, lam_ref, out_ref,
                nat_scr, zp_scr, xh_scr, hf_scr, pf_scr, hb_scr, pb_scr):
    seq = zx_ref.shape[1]
    seg_len = seq // N_SEG
    pitch = seg_len + SUBLANES
    n_blocks = seg_len // T_BLOCK
    blk = T_BLOCK * N_SEG
    taps_after = CONV_WIDTH - 1 - CONV_LEFT

    for s in range(N_SEG):
        nat_scr[s * pitch:s * pitch + seg_len, :] = zx_ref[0, s * seg_len:(s + 1) * seg_len, :]

    def interleave_block(i, _):
        t0 = i * T_BLOCK
        tiles = [nat_scr[pl.ds(t0 + tt, N_SEG, stride=pitch), :] for tt in range(T_BLOCK)]
        dst = pl.multiple_of((t0 + CONV_LEFT) * N_SEG, N_SEG)
        zp_scr[pl.ds(dst, blk), :] = jnp.concatenate(tiles, axis=0)
        return 0

    lax.fori_loop(0, n_blocks, interleave_block, 0)
    for k in range(CONV_LEFT):
        src = (seg_len + k) * N_SEG
        zp_scr[k * N_SEG:(k + 1) * N_SEG, :] = _shift_segments(zp_scr[src:src + N_SEG, :], up=False)
    for k in range(taps_after):
        src = (CONV_LEFT + k) * N_SEG
        dst = (CONV_LEFT + seg_len + k) * N_SEG
        zp_scr[dst:dst + N_SEG, :] = _shift_segments(zp_scr[src:src + N_SEG, :], up=True)

    def conv_block(i, _):
        r0 = pl.multiple_of(i * blk, blk)
        win = zp_scr[pl.ds(r0, blk + (CONV_WIDTH - 1) * N_SEG), :]
        xh = cb_ref[...]
        for k in range(CONV_WIDTH):
            xh = xh + win[k * N_SEG:k * N_SEG + blk] * cw_ref[k:k + 1, :]
        xh_scr[pl.ds(r0, blk), :] = xh
        return 0

    lax.fori_loop(0, n_blocks, conv_block, 0)

    neg_lam = -lam_ref[...]
    softplus = jnp.maximum(neg_lam, 0.0) + jnp.log1p(jnp.exp(-jnp.abs(neg_lam)))
    half_rate = (0.5 * LRU_C) * softplus

    def decay_and_input(xh, d):
        cols = slice(d * 2 * HEAD_DIM, (d + 1) * 2 * HEAD_DIM)
        g = jnp.dot(xh.astype(BF16), wg_ref[0, :, cols], preferred_element_type=F32)
        g = jnp.tanh(g + bg_ref[0, :, cols])
        t_r, t_i = g[:, :HEAD_DIM], g[:, HEAD_DIM:]
        neg_log_a = half_rate[d:d + 1] * t_r + half_rate[d:d + 1]
        a = jnp.exp2(neg_log_a * (-LOG2_E))
        m2 = jnp.maximum(jnp.tanh(neg_log_a) * (a * a + 1.0), 0.0)
        mult = jnp.where(m2 > 0.0, m2 * lax.rsqrt(m2), 0.0)
        return a, mult * (t_i * xh + xh)

    def local_scan(a, b, prod, h, reverse):
        order = range(T_BLOCK - 1, -1, -1) if reverse else range(T_BLOCK)
        hs, ps = [None] * T_BLOCK, [None] * T_BLOCK
        for tt in order:
            rows = slice(tt * N_SEG, (tt + 1) * N_SEG)
            h = a[rows] * h + b[rows]
            prod = prod * a[rows]
            hs[tt], ps[tt] = h, prod
        return jnp.concatenate(hs, axis=0), jnp.concatenate(ps, axis=0), prod, h

    def scan_block(i, carry):
        prod_f, h_f, prod_b, h_b = carry
        rows_f = pl.ds(pl.multiple_of(i * blk, blk), blk)
        rows_b = pl.ds(pl.multiple_of((n_blocks - 1 - i) * blk, blk), blk)
        a, b = decay_and_input(xh_scr[rows_f, :], 0)
        hf_scr[rows_f, :], pf_scr[rows_f, :], prod_f, h_f = local_scan(a, b, prod_f, h_f, False)
        a, b = decay_and_input(xh_scr[rows_b, :], 1)
        hb_scr[rows_b, :], pb_scr[rows_b, :], prod_b, h_b = local_scan(a, b, prod_b, h_b, True)
        return prod_f, h_f, prod_b, h_b

    one = jnp.ones((N_SEG, LANES), F32)
    zero = jnp.zeros((N_SEG, LANES), F32)
    prod_f, h_f, prod_b, h_b = lax.fori_loop(0, n_blocks, scan_block, (one, zero, one, zero))

    carry_f = _shift_segments(_tile_scan(prod_f, h_f, reverse=False), up=False)
    carry_b = _shift_segments(_tile_scan(prod_b, h_b, reverse=True), up=True)

    def fix_block(i, _):
        rows = pl.ds(pl.multiple_of(i * blk, blk), blk)
        shape = (T_BLOCK, N_SEG, LANES)
        h = (hf_scr[rows, :].reshape(shape) + pf_scr[rows, :].reshape(shape) * carry_f
             + hb_scr[rows, :].reshape(shape) + pb_scr[rows, :].reshape(shape) * carry_b)
        hf_scr[rows, :] = h.reshape(blk, LANES)
        return 0

    lax.fori_loop(0, n_blocks, fix_block, 0)

    for s in range(N_SEG):
        for t0 in range(0, seg_len, MERGE_ROWS):
            tiles = [hf_scr[pl.ds((t0 + q * SUBLANES) * N_SEG + s, SUBLANES, stride=N_SEG), :]
                     for q in range(MERGE_ROWS // SUBLANES)]
            h = jnp.concatenate(tiles, axis=0)
            rows = slice(s * seg_len + t0, s * seg_len + t0 + MERGE_ROWS)
            merged = ya_ref[0, rows, :] + gb_ref[0, rows, :] * h
            out_ref[0, rows, :] = merged.astype(out_ref.dtype)


def _lru(zx, gb, ya, conv_w, conv_b, w_gates, b_gates, lam):
    bsz, seq, _ = zx.shape
    seg_len = seq // N_SEG
    slab = pl.BlockSpec((1, seq, HEAD_DIM), lambda b, h: (b, 0, h))
    per_head = lambda rows: pl.BlockSpec((rows, HEAD_DIM), lambda b, h: (0, h))
    seq_scratch = pltpu.VMEM((seq, HEAD_DIM), F32)
    return pl.pallas_call(
        _lru_kernel,
        grid=(bsz, HEADS),
        in_specs=[
            slab, slab, slab,
            per_head(CONV_WIDTH),
            per_head(1),
            pl.BlockSpec((1, HEAD_DIM, 4 * HEAD_DIM), lambda b, h: (h, 0, 0)),
            pl.BlockSpec((1, 1, 4 * HEAD_DIM), lambda b, h: (h, 0, 0)),
            per_head(N_DIRS),
        ],
        out_specs=slab,
        out_shape=jax.ShapeDtypeStruct(zx.shape, BF16),
        scratch_shapes=[
            pltpu.VMEM((N_SEG * (seg_len + SUBLANES), HEAD_DIM), F32),
            pltpu.VMEM(((seg_len + CONV_WIDTH - 1) * N_SEG, HEAD_DIM), F32),
            seq_scratch, seq_scratch, seq_scratch, seq_scratch, seq_scratch,
        ],
        compiler_params=pltpu.CompilerParams(
            dimension_semantics=("arbitrary", "arbitrary"), vmem_limit_bytes=VMEM_LIMIT),
        name="rglru_merge",
    )(zx, gb, ya, conv_w, conv_b, w_gates, b_gates, lam)


def _ffn_kernel(x_ref, m_ref, wo_ref, g2_ref, wi_ref, wf_ref, fg_ref, o_ref, ff_scr, *, final):
    x1 = x_ref[...] + jnp.dot(m_ref[...], wo_ref[...], preferred_element_type=F32)
    o_ref[...] = x1
    hb = (_rms_scale(x1) * g2_ref[...]).astype(BF16)
    for j in range(D_FF // FF_COLS):
        cols = slice(j * FF_COLS, (j + 1) * FF_COLS)
        gate = jnp.dot(hb, wi_ref[:, cols], preferred_element_type=F32)
        up = jnp.dot(hb, wi_ref[:, D_FF + j * FF_COLS:D_FF + (j + 1) * FF_COLS],
                     preferred_element_type=F32)
        ff_scr[:, cols] = (gate * _sigmoid(gate) * up).astype(BF16)
    x2 = o_ref[...] + jnp.dot(ff_scr[...], wf_ref[...], preferred_element_type=F32)
    if final:
        x2 = _rms_scale(x2) * fg_ref[...]
    o_ref[...] = x2


def _ffn(x, merged, w_out, g2, w_ffn_in, w_ffn_out, final_g, final):
    t = x.shape[0]
    row_spec = pl.BlockSpec((ROW_TILE, D_MODEL), lambda i: (i, 0))
    return pl.pallas_call(
        functools.partial(_ffn_kernel, final=final),
        grid=(t // ROW_TILE,),
        in_specs=[
            row_spec, row_spec,
            _const_spec(w_out.shape),
            _const_spec((1, D_MODEL)),
            _const_spec(w_ffn_in.shape),
            _const_spec(w_ffn_out.shape),
            _const_spec((1, D_MODEL)),
        ],
        out_specs=row_spec,
        out_shape=jax.ShapeDtypeStruct((t, D_MODEL), F32),
        scratch_shapes=[pltpu.VMEM((ROW_TILE, D_FF), BF16)],
        compiler_params=pltpu.CompilerParams(
            dimension_semantics=("arbitrary",), vmem_limit_bytes=VMEM_LIMIT),
        name="out_proj_ffn",
    )(x, merged, w_out, g2, w_ffn_in, w_ffn_out, final_g)


def kernel(x, norm1_g, w_in, gmlp_ln_g, gmlp_ln_b, gmlp_w_s, gmlp_b_s, conv_w, conv_b,
           lru_w_r, lru_b_r, lru_w_i, lru_b_i, lru_lambda, w_out, norm2_g, w_ffn_in,
           w_ffn_out, final_g):
    bsz, seq, d = x.shape
    depth = w_in.shape[0]
    assert d == D_MODEL and ROW_TILE % CHUNK == 0 and seq % ROW_TILE == 0
    assert seq % (N_SEG * T_BLOCK) == 0 and (seq // N_SEG) % MERGE_ROWS == 0
    xf = x.reshape(bsz * seq, d)
    row = lambda v: v.reshape(1, -1)
    for l in range(depth):
        b_s = jnp.broadcast_to(gmlp_b_s[l][:, :, None], (GROUPS, CHUNK, GROUP_DIM))
        ya, zx, gb = _proj_in(xf, row(norm1_g[l]), w_in[l].astype(BF16), row(gmlp_ln_g[l]),
                              row(gmlp_ln_b[l]), gmlp_w_s[l].astype(BF16), b_s)
        w_gates = jnp.concatenate(
            [lru_w_r[l, 0], lru_w_i[l, 0], lru_w_r[l, 1], lru_w_i[l, 1]], axis=-1).astype(BF16)
        b_gates = 0.5 * jnp.stack(
            [lru_b_r[l, 0], lru_b_i[l, 0], lru_b_r[l, 1], lru_b_i[l, 1]], axis=0
        ).reshape(4, HEADS, HEAD_DIM).transpose(1, 0, 2).reshape(HEADS, 1, 4 * HEAD_DIM)
        shape3 = (bsz, seq, d)
        merged = _lru(zx.reshape(shape3), gb.reshape(shape3), ya.reshape(shape3),
                      0.5 * conv_w[l], row(0.5 * conv_b[l]), w_gates, b_gates, lru_lambda[l])
        xf = _ffn(xf, merged.reshape(bsz * seq, d), w_out[l].astype(BF16), row(norm2_g[l]),
                  w_ffn_in[l].astype(BF16), w_ffn_out[l].astype(BF16), row(final_g),
                  final=(l == depth - 1))
    return xf.reshape(bsz, seq, d)
```

```python
import functools

import jax
import jax.numpy as jnp
from jax import lax
from jax.experimental import pallas as pl
from jax.experimental.pallas import tpu as pltpu

D_MODEL = 1024
CHUNK = 128
GROUPS = 8
GROUP_DIM = D_MODEL // GROUPS
HEADS = 8
HEAD_DIM = D_MODEL // HEADS
CONV_WIDTH = 4
CONV_LEFT = 1
LRU_C = 8.0
N_DIRS = 2
D_FF = 2816
EPS = 1e-6

SUBLANES = 8
LANES = 128
ROW_TILE = 512
PROJ_COLS = 256
PROJ_ROWS = 256
FF_COLS = 256
N_SEG = SUBLANES
T_BLOCK = 128
MERGE_ROWS = 256
LOG2_E = 1.4426950408889634
VMEM_LIMIT = 56 * 1024 * 1024

F32 = jnp.float32
BF16 = jnp.bfloat16


GELU_C0 = 0.7978845608028654
GELU_C1 = GELU_C0 * 0.044715


def _gelu_tanh(x):
    return jnp.tanh(x * (GELU_C1 * (x * x) + GELU_C0))


def _gelu(x):
    hx = 0.5 * x
    return hx * _gelu_tanh(x) + hx


def _gated_gelu(half_gate, x):
    qx = 0.25 * x
    return (qx * _gelu_tanh(x) + qx) * (jnp.tanh(half_gate) + 1.0)


def _sigmoid(x):
    return 1.0 / (1.0 + jnp.exp(-x))


def _rms_scale(x):
    return x * lax.rsqrt(jnp.mean(x * x, axis=-1, keepdims=True) + EPS)


def _const_spec(shape):
    return pl.BlockSpec(shape, lambda *_: (0,) * len(shape), pipeline_mode=pl.Buffered(1))


def _layer_spec(stacked_shape, layer):
    zeros = (0,) * (len(stacked_shape) - 1)
    return pl.BlockSpec((None,) + tuple(stacked_shape[1:]), lambda *_: (layer,) + zeros,
                        pipeline_mode=pl.Buffered(1))


def _proj_in_kernel(x_ref, g1_ref, w_ref, lng_ref, lnb_ref, ws_ref, bs_ref,
                    ya_ref, zx_ref, gb_ref, gv_scr, vn_scr):
    hb = (_rms_scale(x_ref[...]) * g1_ref[...]).astype(BF16)
    n_chunks = D_MODEL // PROJ_COLS
    col_u, col_v, col_x, col_g, col_gate_a, col_gate_b = range(6)

    def proj(block, tile):
        rows, cols = tile
        c0 = block * D_MODEL + cols.start
        return jnp.dot(hb[rows], w_ref[:, c0:c0 + PROJ_COLS], preferred_element_type=F32)

    tiles = [(slice(r, r + PROJ_ROWS), slice(c, c + PROJ_COLS))
             for c in range(0, D_MODEL, PROJ_COLS) for r in range(0, ROW_TILE, PROJ_ROWS)]
    for tile in tiles:
        gv_scr[tile] = _gelu(proj(col_v, tile))
    for tile in tiles:
        zx_ref[tile] = proj(col_x, tile)

    gv = gv_scr[...]
    dv = gv - jnp.mean(gv, axis=-1, keepdims=True)
    vn = dv * lax.rsqrt(jnp.mean(dv * dv, axis=-1, keepdims=True) + EPS)
    vn_scr[...] = (vn * lng_ref[...] + lnb_ref[...]).astype(BF16)

    n_row_chunks = ROW_TILE // CHUNK
    for g in range(GROUPS):
        cols = slice(g * GROUP_DIM, (g + 1) * GROUP_DIM)
        v_g = jnp.concatenate(
            [vn_scr[c * CHUNK:(c + 1) * CHUNK, cols] for c in range(n_row_chunks)], axis=1)
        mixed = jnp.dot(ws_ref[g], v_g, preferred_element_type=F32)
        for c in range(n_row_chunks):
            ya_ref[c * CHUNK:(c + 1) * CHUNK, cols] = (
                mixed[:, c * GROUP_DIM:(c + 1) * GROUP_DIM] + bs_ref[g])

    for tile in tiles:
        gb_ref[tile] = _gated_gelu(proj(col_gate_b, tile), proj(col_g, tile))
    for tile in tiles:
        ya_ref[tile] = ya_ref[tile] * _gated_gelu(proj(col_gate_a, tile), proj(col_u, tile))


def _proj_in(x, layer, g1, w_in, ln_g, ln_b, w_s, b_s):
    t = x.shape[0]
    row_spec = pl.BlockSpec((ROW_TILE, D_MODEL), lambda i: (i, 0))
    out = jax.ShapeDtypeStruct((t, D_MODEL), F32)
    return pl.pallas_call(
        _proj_in_kernel,
        grid=(t // ROW_TILE,),
        in_specs=[
            row_spec,
            _const_spec((1, D_MODEL)),
            _layer_spec(w_in.shape, layer),
            _const_spec((1, D_MODEL)),
            _const_spec((1, D_MODEL)),
            _layer_spec(w_s.shape, layer),
            _const_spec(b_s.shape),
        ],
        out_specs=[row_spec, row_spec, row_spec],
        out_shape=[out, out, out],
        scratch_shapes=[pltpu.VMEM((ROW_TILE, D_MODEL), F32),
                        pltpu.VMEM((ROW_TILE, D_MODEL), BF16)],
        compiler_params=pltpu.CompilerParams(
            dimension_semantics=("arbitrary",), vmem_limit_bytes=VMEM_LIMIT),
        name="proj_in",
    )(x, g1, w_in, ln_g, ln_b, w_s, b_s)


def _tile_scan(a, b, reverse):
    pos = lax.broadcasted_iota(jnp.int32, a.shape, 0)
    step = 1
    while step < N_SEG:
        if reverse:
            shift, valid = N_SEG - step, pos < N_SEG - step
        else:
            shift, valid = step, pos >= step
        a_nb = jnp.where(valid, pltpu.roll(a, shift, 0), 1.0)
        b_nb = jnp.where(valid, pltpu.roll(b, shift, 0), 0.0)
        b = a * b_nb + b
        a = a * a_nb
        step *= 2
    return b


def _shift_segments(v, up):
    pos = lax.broadcasted_iota(jnp.int32, v.shape, 0)
    if up:
        return jnp.where(pos < N_SEG - 1, pltpu.roll(v, N_SEG - 1, 0), 0.0)
    return jnp.where(pos >= 1, pltpu.roll(v, 1, 0), 0.0)


def _lru_kernel(zx_ref, gb_ref, ya_ref, cw_ref, cb_ref, wg_ref, bg_ref, lam_ref, out_ref,
                nat_scr, zp_scr, xh_scr, hf_scr, pf_scr, hb_scr, pb_scr):
    seq = zx_ref.shape[1]
    seg_len = seq // N_SEG
    pitch = seg_len + SUBLANES
    n_blocks = seg_len // T_BLOCK
    blk = T_BLOCK * N_SEG
    taps_after = CONV_WIDTH - 1 - CONV_LEFT

    for s in range(N_SEG):
        nat_scr[s * pitch:s * pitch + seg_len, :] = zx_ref[0, s * seg_len:(s + 1) * seg_len, :]

    def interleave_block(i, _):
        t0 = i * T_BLOCK
        tiles = [nat_scr[pl.ds(t0 + tt, N_SEG, stride=pitch), :] for tt in range(T_BLOCK)]
        dst = pl.multiple_of((t0 + CONV_LEFT) * N_SEG, N_SEG)
        zp_scr[pl.ds(dst, blk), :] = jnp.concatenate(tiles, axis=0)
        return 0

    lax.fori_loop(0, n_blocks, interleave_block, 0)
    for k in range(CONV_LEFT):
        src = (seg_len + k) * N_SEG
        zp_scr[k * N_SEG:(k + 1) * N_SEG, :] = _shift_segments(zp_scr[src:src + N_SEG, :], up=False)
    for k in range(taps_after):
        src = (CONV_LEFT + k) * N_SEG
        dst = (CONV_LEFT + seg_len + k) * N_SEG
        zp_scr[dst:dst + N_SEG, :] = _shift_segments(zp_scr[src:src + N_SEG, :], up=True)

    def conv_block(i, _):
        r0 = pl.multiple_of(i * blk, blk)
        win = zp_scr[pl.ds(r0, blk + (CONV_WIDTH - 1) * N_SEG), :]
        xh = cb_ref[...]
        for k in range(CONV_WIDTH):
            xh = xh + win[k * N_SEG:k * N_SEG + blk] * cw_ref[k:k + 1, :]
        xh_scr[pl.ds(r0, blk), :] = xh
        return 0

    lax.fori_loop(0, n_blocks, conv_block, 0)

    neg_lam = -lam_ref[...]
    softplus = jnp.maximum(neg_lam, 0.0) + jnp.log1p(jnp.exp(-jnp.abs(neg_lam)))
    half_rate = (0.5 * LRU_C) * softplus

    def decay_and_input(xh, d):
        cols = slice(d * 2 * HEAD_DIM, (d + 1) * 2 * HEAD_DIM)
        g = jnp.dot(xh.astype(BF16), wg_ref[0, :, cols], preferred_element_type=F32)
        g = jnp.tanh(g + bg_ref[0, :, cols])
        t_r, t_i = g[:, :HEAD_DIM], g[:, HEAD_DIM:]
        neg_log_a = half_rate[d:d + 1] * t_r + half_rate[d:d + 1]
        a = jnp.exp2(neg_log_a * (-LOG2_E))
        m2 = jnp.maximum(jnp.tanh(neg_log_a) * (a * a + 1.0), 0.0)
        mult = jnp.where(m2 > 0.0, m2 * lax.rsqrt(m2), 0.0)
        return a, mult * (t_i * xh + xh)

    def local_scan(a, b, prod, h, reverse):
        order = range(T_BLOCK - 1, -1, -1) if reverse else range(T_BLOCK)
        hs, ps = [None] * T_BLOCK, [None] * T_BLOCK
        for tt in order:
            rows = slice(tt * N_SEG, (tt + 1) * N_SEG)
            h = a[rows] * h + b[rows]
            prod = prod * a[rows]
            hs[tt], ps[tt] = h, prod
        return jnp.concatenate(hs, axis=0), jnp.concatenate(ps, axis=0), prod, h

    def scan_block(i, carry):
        prod_f, h_f, prod_b, h_b = carry
        rows_f = pl.ds(pl.multiple_of(i * blk, blk), blk)
        rows_b = pl.ds(pl.multiple_of((n_blocks - 1 - i) * blk, blk), blk)
        a, b = decay_and_input(xh_scr[rows_f, :], 0)
        hf_scr[rows_f, :], pf_scr[rows_f, :], prod_f, h_f = local_scan(a, b, prod_f, h_f, False)
        a, b = decay_and_input(xh_scr[rows_b, :], 1)
        hb_scr[rows_b, :], pb_scr[rows_b, :], prod_b, h_b = local_scan(a, b, prod_b, h_b, True)
        return prod_f, h_f, prod_b, h_b

    one = jnp.ones((N_SEG, LANES), F32)
    zero = jnp.zeros((N_SEG, LANES), F32)
    prod_f, h_f, prod_b, h_b = lax.fori_loop(0, n_blocks, scan_block, (one, zero, one, zero))

    carry_f = _shift_segments(_tile_scan(prod_f, h_f, reverse=False), up=False)
    carry_b = _shift_segments(_tile_scan(prod_b, h_b, reverse=True), up=True)

    def fix_block(i, _):
        rows = pl.ds(pl.multiple_of(i * blk, blk), blk)
        shape = (T_BLOCK, N_SEG, LANES)
        h = (hf_scr[rows, :].reshape(shape) + pf_scr[rows, :].reshape(shape) * carry_f
             + hb_scr[rows, :].reshape(shape) + pb_scr[rows, :].reshape(shape) * carry_b)
        hf_scr[rows, :] = h.reshape(blk, LANES)
        return 0

    lax.fori_loop(0, n_blocks, fix_block, 0)

    for s in range(N_SEG):
        for t0 in range(0, seg_len, MERGE_ROWS):
            tiles = [hf_scr[pl.ds((t0 + q * SUBLANES) * N_SEG + s, SUBLANES, stride=N_SEG), :]
                     for q in range(MERGE_ROWS // SUBLANES)]
            h = jnp.concatenate(tiles, axis=0)
            rows = slice(s * seg_len + t0, s * seg_len + t0 + MERGE_ROWS)
            merged = ya_ref[0, rows, :] + gb_ref[0, rows, :] * h
            out_ref[0, rows, :] = merged.astype(out_ref.dtype)


def _lru(zx, gb, ya, conv_w, conv_b, w_gates, b_gates, lam):
    bsz, seq, _ = zx.shape
    seg_len = seq // N_SEG
    slab = pl.BlockSpec((1, seq, HEAD_DIM), lambda b, h: (b, 0, h))
    per_head = lambda rows: pl.BlockSpec((rows, HEAD_DIM), lambda b, h: (0, h))
    seq_scratch = pltpu.VMEM((seq, HEAD_DIM), F32)
    return pl.pallas_call(
        _lru_kernel,
        grid=(bsz, HEADS),
        in_specs=[
            slab, slab, slab,
            per_head(CONV_WIDTH),
            per_head(1),
            pl.BlockSpec((1, HEAD_DIM, 4 * HEAD_DIM), lambda b, h: (h, 0, 0)),
            pl.BlockSpec((1, 1, 4 * HEAD_DIM), lambda b, h: (h, 0, 0)),
            per_head(N_DIRS),
        ],
        out_specs=slab,
        out_shape=jax.ShapeDtypeStruct(zx.shape, BF16),
        scratch_shapes=[
            pltpu.VMEM((N_SEG * (seg_len + SUBLANES), HEAD_DIM), F32),
            pltpu.VMEM(((seg_len + CONV_WIDTH - 1) * N_SEG, HEAD_DIM), F32),
            seq_scratch, seq_scratch, seq_scratch, seq_scratch, seq_scratch,
        ],
        compiler_params=pltpu.CompilerParams(
            dimension_semantics=("arbitrary", "arbitrary"), vmem_limit_bytes=VMEM_LIMIT),
        name="rglru_merge",
    )(zx, gb, ya, conv_w, conv_b, w_gates, b_gates, lam)


def _ffn_kernel(x_ref, m_ref, wo_ref, g2_ref, wi_ref, wf_ref, fg_ref, o_ref, ff_scr, *, final):
    x1 = x_ref[...] + jnp.dot(m_ref[...], wo_ref[...], preferred_element_type=F32)
    o_ref[...] = x1
    hb = (_rms_scale(x1) * g2_ref[...]).astype(BF16)
    for j in range(D_FF // FF_COLS):
        cols = slice(j * FF_COLS, (j + 1) * FF_COLS)
        gate = jnp.dot(hb, wi_ref[:, cols], preferred_element_type=F32)
        up = jnp.dot(hb, wi_ref[:, D_FF + j * FF_COLS:D_FF + (j + 1) * FF_COLS],
                     preferred_element_type=F32)
        ff_scr[:, cols] = (gate * _sigmoid(gate) * up).astype(BF16)
    x2 = o_ref[...] + jnp.dot(ff_scr[...], wf_ref[...], preferred_element_type=F32)
    if final:
        x2 = _rms_scale(x2) * fg_ref[...]
    o_ref[...] = x2


def _ffn(x, merged, layer, w_out, g2, w_ffn_in, w_ffn_out, final_g, final):
    t = x.shape[0]
    row_spec = pl.BlockSpec((ROW_TILE, D_MODEL), lambda i: (i, 0))
    return pl.pallas_call(
        functools.partial(_ffn_kernel, final=final),
        grid=(t // ROW_TILE,),
        in_specs=[
            row_spec, row_spec,
            _layer_spec(w_out.shape, layer),
            _const_spec((1, D_MODEL)),
            _layer_spec(w_ffn_in.shape, layer),
            _layer_spec(w_ffn_out.shape, layer),
            _const_spec((1, D_MODEL)),
        ],
        out_specs=row_spec,
        out_shape=jax.ShapeDtypeStruct((t, D_MODEL), F32),
        scratch_shapes=[pltpu.VMEM((ROW_TILE, D_FF), BF16)],
        compiler_params=pltpu.CompilerParams(
            dimension_semantics=("arbitrary",), vmem_limit_bytes=VMEM_LIMIT),
        name="out_proj_ffn",
    )(x, merged, w_out, g2, w_ffn_in, w_ffn_out, final_g)


def kernel(x, norm1_g, w_in, gmlp_ln_g, gmlp_ln_b, gmlp_w_s, gmlp_b_s, conv_w, conv_b,
           lru_w_r, lru_b_r, lru_w_i, lru_b_i, lru_lambda, w_out, norm2_g, w_ffn_in,
           w_ffn_out, final_g):
    bsz, seq, d = x.shape
    depth = w_in.shape[0]
    assert d == D_MODEL and ROW_TILE % CHUNK == 0 and seq % ROW_TILE == 0
    assert seq % (N_SEG * T_BLOCK) == 0 and (seq // N_SEG) % MERGE_ROWS == 0
    xf = x.reshape(bsz * seq, d)
    row = lambda v: v.reshape(1, -1)
    w_in = w_in * jnp.concatenate([jnp.ones((4 * d,), F32), jnp.full((2 * d,), 0.5, F32)])
    w_in, gmlp_w_s, w_out, w_ffn_in, w_ffn_out = (
        w.astype(BF16) for w in (w_in, gmlp_w_s, w_out, w_ffn_in, w_ffn_out))
    for l in range(depth):
        b_s = jnp.broadcast_to(gmlp_b_s[l][:, :, None], (GROUPS, CHUNK, GROUP_DIM))
        ya, zx, gb = _proj_in(xf, l, row(norm1_g[l]), w_in, row(gmlp_ln_g[l]),
                              row(gmlp_ln_b[l]), gmlp_w_s, b_s)
        w_gates = jnp.concatenate(
            [lru_w_r[l, 0], lru_w_i[l, 0], lru_w_r[l, 1], lru_w_i[l, 1]], axis=-1).astype(BF16)
        b_gates = 0.5 * jnp.stack(
            [lru_b_r[l, 0], lru_b_i[l, 0], lru_b_r[l, 1], lru_b_i[l, 1]], axis=0
        ).reshape(4, HEADS, HEAD_DIM).transpose(1, 0, 2).reshape(HEADS, 1, 4 * HEAD_DIM)
        shape3 = (bsz, seq, d)
        merged = _lru(zx.reshape(shape3), gb.reshape(shape3), ya.reshape(shape3),
                      0.5 * conv_w[l], row(0.5 * conv_b[l]), w_gates, b_gates, lru_lambda[l])
        xf = _ffn(xf, merged.reshape(bsz * seq, d), l, w_out, row(norm2_g[l]),
                  w_ffn_in, w_ffn_out, row(final_g), final=(l == depth - 1))
    return xf.reshape(bsz, seq, d)
```

```python
import functools

import jax
import jax.numpy as jnp
from jax import lax
from jax.experimental import pallas as pl
from jax.experimental.pallas import tpu as pltpu

D_MODEL = 1024
CHUNK = 128
GROUPS = 8
GROUP_DIM = D_MODEL // GROUPS
HEADS = 8
HEAD_DIM = D_MODEL // HEADS
CONV_WIDTH = 4
CONV_LEFT = 1
LRU_C = 8.0
N_DIRS = 2
D_FF = 2816
EPS = 1e-6

SUBLANES = 8
LANES = 128
ROW_TILE = 1024
PROJ_COLS = 256
PROJ_ROWS = 256
FF_COLS = 256
N_SEG = SUBLANES
T_BLOCK = 128
MERGE_ROWS = 256
LOG2_E = 1.4426950408889634
VMEM_LIMIT = 56 * 1024 * 1024

F32 = jnp.float32
BF16 = jnp.bfloat16


GELU_C0 = 0.7978845608028654
GELU_C1 = GELU_C0 * 0.044715


def _gelu_tanh(x):
    return jnp.tanh(x * (GELU_C1 * (x * x) + GELU_C0))


def _gelu(x):
    hx = 0.5 * x
    return hx * _gelu_tanh(x) + hx


def _gated_gelu(half_gate, x):
    qx = 0.25 * x
    return (qx * _gelu_tanh(x) + qx) * (jnp.tanh(half_gate) + 1.0)


def _sigmoid(x):
    return 1.0 / (1.0 + jnp.exp(-x))


def _rms_scale(x):
    return x * lax.rsqrt(jnp.mean(x * x, axis=-1, keepdims=True) + EPS)


def _const_spec(shape):
    return pl.BlockSpec(shape, lambda *_: (0,) * len(shape), pipeline_mode=pl.Buffered(1))


def _layer_spec(stacked_shape, layer):
    zeros = (0,) * (len(stacked_shape) - 1)
    return pl.BlockSpec((None,) + tuple(stacked_shape[1:]), lambda *_: (layer,) + zeros,
                        pipeline_mode=pl.Buffered(1))


def _proj_in_kernel(x_ref, g1_ref, w_ref, lng_ref, lnb_ref, ws_ref, bs_ref,
                    ya_ref, zx_ref, gb_ref, gv_scr, vn_scr):
    hb = (_rms_scale(x_ref[...]) * g1_ref[...]).astype(BF16)
    col_u, col_v, col_x, col_g, col_gate_a, col_gate_b = range(6)

    def proj(block, tile):
        rows, cols = tile
        c0 = block * D_MODEL + cols.start
        return jnp.dot(hb[rows], w_ref[:, c0:c0 + PROJ_COLS], preferred_element_type=F32)

    col_starts = range(0, D_MODEL, PROJ_COLS)
    tiles = [(slice(r, r + PROJ_ROWS), slice(c, c + PROJ_COLS))
             for c in col_starts for r in range(0, ROW_TILE, PROJ_ROWS)]
    for tile in tiles:
        gv_scr[tile] = _gelu(proj(col_v, tile))
    for tile in tiles:
        zx_ref[tile] = proj(col_x, tile)

    def layer_norm(rows):
        gv = gv_scr[rows, :]
        dv = gv - jnp.mean(gv, axis=-1, keepdims=True)
        vn = dv * lax.rsqrt(jnp.mean(dv * dv, axis=-1, keepdims=True) + EPS)
        vn_scr[rows, :] = (vn * lng_ref[...] + lnb_ref[...]).astype(BF16)

    ln_rows = ROW_TILE // len(tiles)
    for k, tile in enumerate(tiles):
        layer_norm(slice(k * ln_rows, (k + 1) * ln_rows))
        gb_ref[tile] = _gated_gelu(proj(col_gate_b, tile), proj(col_g, tile))

    def spatial_mix(g):
        cols = slice(g * GROUP_DIM, (g + 1) * GROUP_DIM)
        n_row_chunks = ROW_TILE // CHUNK
        v_g = jnp.concatenate(
            [vn_scr[c * CHUNK:(c + 1) * CHUNK, cols] for c in range(n_row_chunks)], axis=1)
        mixed = jnp.dot(ws_ref[g], v_g, preferred_element_type=F32)
        for c in range(n_row_chunks):
            ya_ref[c * CHUNK:(c + 1) * CHUNK, cols] = (
                mixed[:, c * GROUP_DIM:(c + 1) * GROUP_DIM] + bs_ref[g])

    for tile in tiles:
        rows, cols = tile
        if rows.start == 0:
            for g in range(cols.start // GROUP_DIM, cols.stop // GROUP_DIM):
                spatial_mix(g)
        ya_ref[tile] = ya_ref[tile] * _gated_gelu(proj(col_gate_a, tile), proj(col_u, tile))


def _proj_in(x, layer, g1, w_in, ln_g, ln_b, w_s, b_s):
    t = x.shape[0]
    row_spec = pl.BlockSpec((ROW_TILE, D_MODEL), lambda i: (i, 0))
    out = jax.ShapeDtypeStruct((t, D_MODEL), F32)
    return pl.pallas_call(
        _proj_in_kernel,
        grid=(t // ROW_TILE,),
        in_specs=[
            row_spec,
            _const_spec((1, D_MODEL)),
            _layer_spec(w_in.shape, layer),
            _const_spec((1, D_MODEL)),
            _const_spec((1, D_MODEL)),
            _layer_spec(w_s.shape, layer),
            _const_spec(b_s.shape),
        ],
        out_specs=[row_spec, row_spec, row_spec],
        out_shape=[out, out, out],
        scratch_shapes=[pltpu.VMEM((ROW_TILE, D_MODEL), F32),
                        pltpu.VMEM((ROW_TILE, D_MODEL), BF16)],
        compiler_params=pltpu.CompilerParams(
            dimension_semantics=("arbitrary",), vmem_limit_bytes=VMEM_LIMIT),
        name="proj_in",
    )(x, g1, w_in, ln_g, ln_b, w_s, b_s)


def _tile_scan(a, b, reverse):
    pos = lax.broadcasted_iota(jnp.int32, a.shape, 0)
    step = 1
    while step < N_SEG:
        if reverse:
            shift, valid = N_SEG - step, pos < N_SEG - step
        else:
            shift, valid = step, pos >= step
        a_nb = jnp.where(valid, pltpu.roll(a, shift, 0), 1.0)
        b_nb = jnp.where(valid, pltpu.roll(b, shift, 0), 0.0)
        b = a * b_nb + b
        a = a * a_nb
        step *= 2
    return b


def _shift_segments(v, up):
    pos = lax.broadcasted_iota(jnp.int32, v.shape, 0)
    if up:
        return jnp.where(pos < N_SEG - 1, pltpu.roll(v, N_SEG - 1, 0), 0.0)
    return jnp.where(pos >= 1, pltpu.roll(v, 1, 0), 0.0)


def _lru_kernel(zx_ref, gb_ref, ya_ref, cw_ref, cb_ref, wg_ref, bg_ref, lam_ref, out_ref,
                nat_scr, zp_scr, xh_scr, hf_scr, pf_scr, hb_scr, pb_scr):
    seq = zx_ref.shape[1]
    seg_len = seq // N_SEG
    pitch = seg_len + SUBLANES
    n_blocks = seg_len // T_BLOCK
    blk = T_BLOCK * N_SEG
    taps_after = CONV_WIDTH - 1 - CONV_LEFT

    for s in range(N_SEG):
        nat_scr[s * pitch:s * pitch + seg_len, :] = zx_ref[0, s * seg_len:(s + 1) * seg_len, :]

    def interleave_block(i, _):
        t0 = i * T_BLOCK
        tiles = [nat_scr[pl.ds(t0 + tt, N_SEG, stride=pitch), :] for tt in range(T_BLOCK)]
        dst = pl.multiple_of((t0 + CONV_LEFT) * N_SEG, N_SEG)
        zp_scr[pl.ds(dst, blk), :] = jnp.concatenate(tiles, axis=0)
        return 0

    lax.fori_loop(0, n_blocks, interleave_block, 0)
    for k in range(CONV_LEFT):
        src = (seg_len + k) * N_SEG
        zp_scr[k * N_SEG:(k + 1) * N_SEG, :] = _shift_segments(zp_scr[src:src + N_SEG, :], up=False)
    for k in range(taps_after):
        src = (CONV_LEFT + k) * N_SEG
        dst = (CONV_LEFT + seg_len + k) * N_SEG
        zp_scr[dst:dst + N_SEG, :] = _shift_segments(zp_scr[src:src + N_SEG, :], up=True)

    def conv_block(i, _):
        r0 = pl.multiple_of(i * blk, blk)
        win = zp_scr[pl.ds(r0, blk + (CONV_WIDTH - 1) * N_SEG), :]
        xh = cb_ref[...]
        for k in range(CONV_WIDTH):
            xh = xh + win[k * N_SEG:k * N_SEG + blk] * cw_ref[k:k + 1, :]
        xh_scr[pl.ds(r0, blk), :] = xh
        return 0

    lax.fori_loop(0, n_blocks, conv_block, 0)

    neg_lam = -lam_ref[...]
    softplus = jnp.maximum(neg_lam, 0.0) + jnp.log1p(jnp.exp(-jnp.abs(neg_lam)))
    half_rate = (0.5 * LRU_C) * softplus

    def decay_and_input(xh, d):
        cols = slice(d * 2 * HEAD_DIM, (d + 1) * 2 * HEAD_DIM)
        g = jnp.dot(xh.astype(BF16), wg_ref[0, :, cols], preferred_element_type=F32)
        g = jnp.tanh(g + bg_ref[0, :, cols])
        t_r, t_i = g[:, :HEAD_DIM], g[:, HEAD_DIM:]
        neg_log_a = half_rate[d:d + 1] * t_r + half_rate[d:d + 1]
        a = jnp.exp2(neg_log_a * (-LOG2_E))
        m2 = jnp.maximum(jnp.tanh(neg_log_a) * (a * a + 1.0), 0.0)
        mult = jnp.where(m2 > 0.0, m2 * lax.rsqrt(m2), 0.0)
        return a, mult * (t_i * xh + xh)

    def local_scan(a, b, prod, h, reverse):
        order = range(T_BLOCK - 1, -1, -1) if reverse else range(T_BLOCK)
        hs, ps = [None] * T_BLOCK, [None] * T_BLOCK
        for tt in order:
            rows = slice(tt * N_SEG, (tt + 1) * N_SEG)
            h = a[rows] * h + b[rows]
            prod = prod * a[rows]
            hs[tt], ps[tt] = h, prod
        return jnp.concatenate(hs, axis=0), jnp.concatenate(ps, axis=0), prod, h

    def scan_block(i, carry):
        prod_f, h_f, prod_b, h_b = carry
        rows_f = pl.ds(pl.multiple_of(i * blk, blk), blk)
        rows_b = pl.ds(pl.multiple_of((n_blocks - 1 - i) * blk, blk), blk)
        a, b = decay_and_input(xh_scr[rows_f, :], 0)
        hf_scr[rows_f, :], pf_scr[rows_f, :], prod_f, h_f = local_scan(a, b, prod_f, h_f, False)
        a, b = decay_and_input(xh_scr[rows_b, :], 1)
        hb_scr[rows_b, :], pb_scr[rows_b, :], prod_b, h_b = local_scan(a, b, prod_b, h_b, True)
        return prod_f, h_f, prod_b, h_b

    one = jnp.ones((N_SEG, LANES), F32)
    zero = jnp.zeros((N_SEG, LANES), F32)
    prod_f, h_f, prod_b, h_b = lax.fori_loop(0, n_blocks, scan_block, (one, zero, one, zero))

    carry_f = _shift_segments(_tile_scan(prod_f, h_f, reverse=False), up=False)
    carry_b = _shift_segments(_tile_scan(prod_b, h_b, reverse=True), up=True)

    def fix_block(i, _):
        rows = pl.ds(pl.multiple_of(i * blk, blk), blk)
        shape = (T_BLOCK, N_SEG, LANES)
        h = (hf_scr[rows, :].reshape(shape) + pf_scr[rows, :].reshape(shape) * carry_f
             + hb_scr[rows, :].reshape(shape) + pb_scr[rows, :].reshape(shape) * carry_b)
        hf_scr[rows, :] = h.reshape(blk, LANES)
        return 0

    lax.fori_loop(0, n_blocks, fix_block, 0)

    for s in range(N_SEG):
        for t0 in range(0, seg_len, MERGE_ROWS):
            tiles = [hf_scr[pl.ds((t0 + q * SUBLANES) * N_SEG + s, SUBLANES, stride=N_SEG), :]
                     for q in range(MERGE_ROWS // SUBLANES)]
            h = jnp.concatenate(tiles, axis=0)
            rows = slice(s * seg_len + t0, s * seg_len + t0 + MERGE_ROWS)
            merged = ya_ref[0, rows, :] + gb_ref[0, rows, :] * h
            out_ref[0, rows, :] = merged.astype(out_ref.dtype)


def _lru(zx, gb, ya, conv_w, conv_b, w_gates, b_gates, lam):
    bsz, seq, _ = zx.shape
    seg_len = seq // N_SEG
    slab = pl.BlockSpec((1, seq, HEAD_DIM), lambda b, h: (b, 0, h))
    per_head = lambda rows: pl.BlockSpec((rows, HEAD_DIM), lambda b, h: (0, h))
    seq_scratch = pltpu.VMEM((seq, HEAD_DIM), F32)
    return pl.pallas_call(
        _lru_kernel,
        grid=(bsz, HEADS),
        in_specs=[
            slab, slab, slab,
            per_head(CONV_WIDTH),
            per_head(1),
            pl.BlockSpec((1, HEAD_DIM, 4 * HEAD_DIM), lambda b, h: (h, 0, 0)),
            pl.BlockSpec((1, 1, 4 * HEAD_DIM), lambda b, h: (h, 0, 0)),
            per_head(N_DIRS),
        ],
        out_specs=slab,
        out_shape=jax.ShapeDtypeStruct(zx.shape, BF16),
        scratch_shapes=[
            pltpu.VMEM((N_SEG * (seg_len + SUBLANES), HEAD_DIM), F32),
            pltpu.VMEM(((seg_len + CONV_WIDTH - 1) * N_SEG, HEAD_DIM), F32),
            seq_scratch, seq_scratch, seq_scratch, seq_scratch, seq_scratch,
        ],
        compiler_params=pltpu.CompilerParams(
            dimension_semantics=("arbitrary", "arbitrary"), vmem_limit_bytes=VMEM_LIMIT),
        name="rglru_merge",
    )(zx, gb, ya, conv_w, conv_b, w_gates, b_gates, lam)


def _ffn_kernel(x_ref, m_ref, wo_ref, g2_ref, wi_ref, wf_ref, fg_ref, o_ref, ff_scr, *, final):
    x1 = x_ref[...] + jnp.dot(m_ref[...], wo_ref[...], preferred_element_type=F32)
    o_ref[...] = x1
    hb = (_rms_scale(x1) * g2_ref[...]).astype(BF16)
    for j in range(D_FF // FF_COLS):
        cols = slice(j * FF_COLS, (j + 1) * FF_COLS)
        gate = jnp.dot(hb, wi_ref[:, cols], preferred_element_type=F32)
        up = jnp.dot(hb, wi_ref[:, D_FF + j * FF_COLS:D_FF + (j + 1) * FF_COLS],
                     preferred_element_type=F32)
        ff_scr[:, cols] = (gate * _sigmoid(gate) * up).astype(BF16)
    x2 = o_ref[...] + jnp.dot(ff_scr[...], wf_ref[...], preferred_element_type=F32)
    if final:
        x2 = _rms_scale(x2) * fg_ref[...]
    o_ref[...] = x2


def _ffn(x, merged, layer, w_out, g2, w_ffn_in, w_ffn_out, final_g, final):
    t = x.shape[0]
    row_spec = pl.BlockSpec((ROW_TILE, D_MODEL), lambda i: (i, 0))
    return pl.pallas_call(
        functools.partial(_ffn_kernel, final=final),
        grid=(t // ROW_TILE,),
        in_specs=[
            row_spec, row_spec,
            _layer_spec(w_out.shape, layer),
            _const_spec((1, D_MODEL)),
            _layer_spec(w_ffn_in.shape, layer),
            _layer_spec(w_ffn_out.shape, layer),
            _const_spec((1, D_MODEL)),
        ],
        out_specs=row_spec,
        out_shape=jax.ShapeDtypeStruct((t, D_MODEL), F32),
        scratch_shapes=[pltpu.VMEM((ROW_TILE, D_FF), BF16)],
        compiler_params=pltpu.CompilerParams(
            dimension_semantics=("arbitrary",), vmem_limit_bytes=VMEM_LIMIT),
        name="out_proj_ffn",
    )(x, merged, w_out, g2, w_ffn_in, w_ffn_out, final_g)


def kernel(x, norm1_g, w_in, gmlp_ln_g, gmlp_ln_b, gmlp_w_s, gmlp_b_s, conv_w, conv_b,
           lru_w_r, lru_b_r, lru_w_i, lru_b_i, lru_lambda, w_out, norm2_g, w_ffn_in,
           w_ffn_out, final_g):
    bsz, seq, d = x.shape
    depth = w_in.shape[0]
    assert d == D_MODEL and ROW_TILE % CHUNK == 0 and seq % ROW_TILE == 0
    assert seq % (N_SEG * T_BLOCK) == 0 and (seq // N_SEG) % MERGE_ROWS == 0
    xf = x.reshape(bsz * seq, d)
    row = lambda v: v.reshape(1, -1)
    w_in = w_in * jnp.concatenate([jnp.ones((4 * d,), F32), jnp.full((2 * d,), 0.5, F32)])
    w_in, gmlp_w_s, w_out, w_ffn_in, w_ffn_out = (
        w.astype(BF16) for w in (w_in, gmlp_w_s, w_out, w_ffn_in, w_ffn_out))
    for l in range(depth):
        b_s = jnp.broadcast_to(gmlp_b_s[l][:, :, None], (GROUPS, CHUNK, GROUP_DIM))
        ya, zx, gb = _proj_in(xf, l, row(norm1_g[l]), w_in, row(gmlp_ln_g[l]),
                              row(gmlp_ln_b[l]), gmlp_w_s, b_s)
        w_gates = jnp.concatenate(
            [lru_w_r[l, 0], lru_w_i[l, 0], lru_w_r[l, 1], lru_w_i[l, 1]], axis=-1).astype(BF16)
        b_gates = 0.5 * jnp.stack(
            [lru_b_r[l, 0], lru_b_i[l, 0], lru_b_r[l, 1], lru_b_i[l, 1]], axis=0
        ).reshape(4, HEADS, HEAD_DIM).transpose(1, 0, 2).reshape(HEADS, 1, 4 * HEAD_DIM)
        shape3 = (bsz, seq, d)
        merged = _lru(zx.reshape(shape3), gb.reshape(shape3), ya.reshape(shape3),
                      0.5 * conv_w[l], row(0.5 * conv_b[l]), w_gates, b_gates, lru_lambda[l])
        xf = _ffn(xf, merged.reshape(bsz * seq, d), l, w_out, row(norm2_g[l]),
                  w_ffn_in, w_ffn_out, row(final_g), final=(l == depth - 1))
    return xf.reshape(bsz, seq, d)
```

```python
import functools

import jax
import jax.numpy as jnp
from jax import lax
from jax.experimental import pallas as pl
from jax.experimental.pallas import tpu as pltpu

D_MODEL = 1024
CHUNK = 128
GROUPS = 8
GROUP_DIM = D_MODEL // GROUPS
HEADS = 8
HEAD_DIM = D_MODEL // HEADS
CONV_WIDTH = 4
CONV_LEFT = 1
LRU_C = 8.0
N_DIRS = 2
D_FF = 2816
EPS = 1e-6

SUBLANES = 8
LANES = 128
ROW_TILE = 1024
PROJ_COLS = 256
PROJ_ROWS = 512
FF_COLS = 256
N_SEG = SUBLANES
T_BLOCK = 128
MERGE_ROWS = 256
LOG2_E = 1.4426950408889634
VMEM_LIMIT = 56 * 1024 * 1024

F32 = jnp.float32
BF16 = jnp.bfloat16


GELU_C0 = 0.7978845608028654
GELU_C1 = GELU_C0 * 0.044715


def _gelu_tanh(x):
    return jnp.tanh(x * (GELU_C1 * (x * x) + GELU_C0))


def _gelu(x):
    hx = 0.5 * x
    return hx * _gelu_tanh(x) + hx


def _gated_gelu(half_gate, x):
    qx = 0.25 * x
    return (qx * _gelu_tanh(x) + qx) * (jnp.tanh(half_gate) + 1.0)


def _sigmoid(x):
    return 1.0 / (1.0 + jnp.exp(-x))


def _rms_scale(x):
    return x * lax.rsqrt(jnp.mean(x * x, axis=-1, keepdims=True) + EPS)


def _const_spec(shape):
    return pl.BlockSpec(shape, lambda *_: (0,) * len(shape), pipeline_mode=pl.Buffered(1))


def _layer_spec(stacked_shape, layer):
    zeros = (0,) * (len(stacked_shape) - 1)
    return pl.BlockSpec((None,) + tuple(stacked_shape[1:]), lambda *_: (layer,) + zeros,
                        pipeline_mode=pl.Buffered(1))


def _proj_in_kernel(x_ref, g1_ref, w_ref, lng_ref, lnb_ref, ws_ref, bs_ref,
                    ya_ref, zx_ref, gb_ref, gv_scr, vn_scr):
    hb = (_rms_scale(x_ref[...]) * g1_ref[...]).astype(BF16)
    col_u, col_v, col_x, col_g, col_gate_a, col_gate_b = range(6)

    def proj(block, tile):
        rows, cols = tile
        c0 = block * D_MODEL + cols.start
        return jnp.dot(hb[rows], w_ref[:, c0:c0 + PROJ_COLS], preferred_element_type=F32)

    col_starts = range(0, D_MODEL, PROJ_COLS)
    tiles = [(slice(r, r + PROJ_ROWS), slice(c, c + PROJ_COLS))
             for c in col_starts for r in range(0, ROW_TILE, PROJ_ROWS)]
    for tile in tiles:
        gv_scr[tile] = _gelu(proj(col_v, tile))
    for tile in tiles:
        zx_ref[tile] = proj(col_x, tile)

    def layer_norm(rows):
        gv = gv_scr[rows, :]
        dv = gv - jnp.mean(gv, axis=-1, keepdims=True)
        vn = dv * lax.rsqrt(jnp.mean(dv * dv, axis=-1, keepdims=True) + EPS)
        vn_scr[rows, :] = (vn * lng_ref[...] + lnb_ref[...]).astype(BF16)

    ln_rows = ROW_TILE // len(tiles)
    for k, tile in enumerate(tiles):
        layer_norm(slice(k * ln_rows, (k + 1) * ln_rows))
        gb_ref[tile] = _gated_gelu(proj(col_gate_b, tile), proj(col_g, tile))

    def spatial_mix(g):
        cols = slice(g * GROUP_DIM, (g + 1) * GROUP_DIM)
        n_row_chunks = ROW_TILE // CHUNK
        v_g = jnp.concatenate(
            [vn_scr[c * CHUNK:(c + 1) * CHUNK, cols] for c in range(n_row_chunks)], axis=1)
        mixed = jnp.dot(ws_ref[g], v_g, preferred_element_type=F32)
        for c in range(n_row_chunks):
            ya_ref[c * CHUNK:(c + 1) * CHUNK, cols] = (
                mixed[:, c * GROUP_DIM:(c + 1) * GROUP_DIM] + bs_ref[g])

    for tile in tiles:
        rows, cols = tile
        if rows.start == 0:
            for g in range(cols.start // GROUP_DIM, cols.stop // GROUP_DIM):
                spatial_mix(g)
        ya_ref[tile] = ya_ref[tile] * _gated_gelu(proj(col_gate_a, tile), proj(col_u, tile))


def _proj_in(x, layer, g1, w_in, ln_g, ln_b, w_s, b_s):
    t = x.shape[0]
    row_spec = pl.BlockSpec((ROW_TILE, D_MODEL), lambda i: (i, 0))
    out = jax.ShapeDtypeStruct((t, D_MODEL), F32)
    return pl.pallas_call(
        _proj_in_kernel,
        grid=(t // ROW_TILE,),
        in_specs=[
            row_spec,
            _const_spec((1, D_MODEL)),
            _layer_spec(w_in.shape, layer),
            _const_spec((1, D_MODEL)),
            _const_spec((1, D_MODEL)),
            _layer_spec(w_s.shape, layer),
            _const_spec(b_s.shape),
        ],
        out_specs=[row_spec, row_spec, row_spec],
        out_shape=[out, out, out],
        scratch_shapes=[pltpu.VMEM((ROW_TILE, D_MODEL), F32),
                        pltpu.VMEM((ROW_TILE, D_MODEL), BF16)],
        compiler_params=pltpu.CompilerParams(
            dimension_semantics=("arbitrary",), vmem_limit_bytes=VMEM_LIMIT),
        name="proj_in",
    )(x, g1, w_in, ln_g, ln_b, w_s, b_s)


def _tile_scan(a, b, reverse):
    pos = lax.broadcasted_iota(jnp.int32, a.shape, 0)
    step = 1
    while step < N_SEG:
        if reverse:
            shift, valid = N_SEG - step, pos < N_SEG - step
        else:
            shift, valid = step, pos >= step
        a_nb = jnp.where(valid, pltpu.roll(a, shift, 0), 1.0)
        b_nb = jnp.where(valid, pltpu.roll(b, shift, 0), 0.0)
        b = a * b_nb + b
        a = a * a_nb
        step *= 2
    return b


def _shift_segments(v, up):
    pos = lax.broadcasted_iota(jnp.int32, v.shape, 0)
    if up:
        return jnp.where(pos < N_SEG - 1, pltpu.roll(v, N_SEG - 1, 0), 0.0)
    return jnp.where(pos >= 1, pltpu.roll(v, 1, 0), 0.0)


def _lru_kernel(zx_ref, gb_ref, ya_ref, cw_ref, cb_ref, wg_ref, bg_ref, lam_ref, out_ref,
                nat_scr, xh_scr, hf_scr, pf_scr, hb_scr, pb_scr):
    seq = zx_ref.shape[1]
    seg_len = seq // N_SEG
    pitch = seg_len + SUBLANES
    n_blocks = seg_len // T_BLOCK
    blk = T_BLOCK * N_SEG
    taps_after = CONV_WIDTH - 1 - CONV_LEFT

    halo = SUBLANES
    zero_rows = jnp.zeros((SUBLANES, LANES), F32)
    for s in range(N_SEG):
        first = s * pitch + halo
        nat_scr[first:first + seg_len, :] = zx_ref[0, s * seg_len:(s + 1) * seg_len, :]
        before = slice(first - CONV_LEFT, first)
        after = slice(first + seg_len, first + seg_len + taps_after)
        if s > 0:
            nat_scr[before, :] = zx_ref[0, s * seg_len - CONV_LEFT:s * seg_len, :]
        else:
            nat_scr[before, :] = zero_rows[:CONV_LEFT]
        if s < N_SEG - 1:
            nat_scr[after, :] = zx_ref[0, (s + 1) * seg_len:(s + 1) * seg_len + taps_after, :]
        else:
            nat_scr[after, :] = zero_rows[:taps_after]

    def conv_block(i, _):
        t0 = i * T_BLOCK
        steps = [nat_scr[pl.ds(t0 + halo - CONV_LEFT + j, N_SEG, stride=pitch), :]
                 for j in range(T_BLOCK + CONV_WIDTH - 1)]
        taps = [jnp.broadcast_to(cw_ref[k:k + 1, :], (N_SEG, LANES)) for k in range(CONV_WIDTH)]
        bias = jnp.broadcast_to(cb_ref[...], (N_SEG, LANES))
        tiles = []
        for tt in range(T_BLOCK):
            xh = bias
            for k in range(CONV_WIDTH):
                xh = xh + steps[tt + k] * taps[k]
            tiles.append(xh)
        xh_scr[pl.ds(pl.multiple_of(i * blk, blk), blk), :] = jnp.concatenate(tiles, axis=0)
        return 0

    lax.fori_loop(0, n_blocks, conv_block, 0)

    neg_lam = -lam_ref[...]
    softplus = jnp.maximum(neg_lam, 0.0) + jnp.log1p(jnp.exp(-jnp.abs(neg_lam)))
    half_rate = (0.5 * LRU_C) * softplus

    def decay_and_input(xh, d):
        cols = slice(d * 2 * HEAD_DIM, (d + 1) * 2 * HEAD_DIM)
        g = jnp.dot(xh.astype(BF16), wg_ref[0, :, cols], preferred_element_type=F32)
        g = jnp.tanh(g + bg_ref[0, :, cols])
        t_r, t_i = g[:, :HEAD_DIM], g[:, HEAD_DIM:]
        neg_log_a = half_rate[d:d + 1] * t_r + half_rate[d:d + 1]
        a = jnp.exp2(neg_log_a * (-LOG2_E))
        m2 = jnp.maximum(jnp.tanh(neg_log_a) * (a * a + 1.0), 0.0)
        mult = jnp.where(m2 > 0.0, m2 * lax.rsqrt(m2), 0.0)
        return a, mult * (t_i * xh + xh)

    def local_scan(a, b, prod, h, reverse):
        order = range(T_BLOCK - 1, -1, -1) if reverse else range(T_BLOCK)
        hs, ps = [None] * T_BLOCK, [None] * T_BLOCK
        for tt in order:
            rows = slice(tt * N_SEG, (tt + 1) * N_SEG)
            h = a[rows] * h + b[rows]
            prod = prod * a[rows]
            hs[tt], ps[tt] = h, prod
        return jnp.concatenate(hs, axis=0), jnp.concatenate(ps, axis=0), prod, h

    def scan_block(i, carry):
        prod_f, h_f, prod_b, h_b = carry
        rows_f = pl.ds(pl.multiple_of(i * blk, blk), blk)
        rows_b = pl.ds(pl.multiple_of((n_blocks - 1 - i) * blk, blk), blk)
        a, b = decay_and_input(xh_scr[rows_f, :], 0)
        hf_scr[rows_f, :], pf_scr[rows_f, :], prod_f, h_f = local_scan(a, b, prod_f, h_f, False)
        a, b = decay_and_input(xh_scr[rows_b, :], 1)
        hb_scr[rows_b, :], pb_scr[rows_b, :], prod_b, h_b = local_scan(a, b, prod_b, h_b, True)
        return prod_f, h_f, prod_b, h_b

    one = jnp.ones((N_SEG, LANES), F32)
    zero = jnp.zeros((N_SEG, LANES), F32)
    prod_f, h_f, prod_b, h_b = lax.fori_loop(0, n_blocks, scan_block, (one, zero, one, zero))

    carry_f = _shift_segments(_tile_scan(prod_f, h_f, reverse=False), up=False)
    carry_b = _shift_segments(_tile_scan(prod_b, h_b, reverse=True), up=True)

    def fix_block(i, _):
        rows = pl.ds(pl.multiple_of(i * blk, blk), blk)
        shape = (T_BLOCK, N_SEG, LANES)
        h = (hf_scr[rows, :].reshape(shape) + pf_scr[rows, :].reshape(shape) * carry_f
             + hb_scr[rows, :].reshape(shape) + pb_scr[rows, :].reshape(shape) * carry_b)
        hf_scr[rows, :] = h.reshape(blk, LANES)
        return 0

    lax.fori_loop(0, n_blocks, fix_block, 0)

    for s in range(N_SEG):
        for t0 in range(0, seg_len, MERGE_ROWS):
            tiles = [hf_scr[pl.ds((t0 + q * SUBLANES) * N_SEG + s, SUBLANES, stride=N_SEG), :]
                     for q in range(MERGE_ROWS // SUBLANES)]
            h = jnp.concatenate(tiles, axis=0)
            rows = slice(s * seg_len + t0, s * seg_len + t0 + MERGE_ROWS)
            merged = ya_ref[0, rows, :] + gb_ref[0, rows, :] * h
            out_ref[0, rows, :] = merged.astype(out_ref.dtype)


def _lru(zx, gb, ya, conv_w, conv_b, w_gates, b_gates, lam):
    bsz, seq, _ = zx.shape
    seg_len = seq // N_SEG
    slab = pl.BlockSpec((1, seq, HEAD_DIM), lambda b, h: (b, 0, h))
    per_head = lambda rows: pl.BlockSpec((rows, HEAD_DIM), lambda b, h: (0, h))
    seq_scratch = pltpu.VMEM((seq, HEAD_DIM), F32)
    return pl.pallas_call(
        _lru_kernel,
        grid=(bsz, HEADS),
        in_specs=[
            slab, slab, slab,
            per_head(CONV_WIDTH),
            per_head(1),
            pl.BlockSpec((1, HEAD_DIM, 4 * HEAD_DIM), lambda b, h: (h, 0, 0)),
            pl.BlockSpec((1, 1, 4 * HEAD_DIM), lambda b, h: (h, 0, 0)),
            per_head(N_DIRS),
        ],
        out_specs=slab,
        out_shape=jax.ShapeDtypeStruct(zx.shape, BF16),
        scratch_shapes=[
            pltpu.VMEM((N_SEG * (seg_len + SUBLANES) + SUBLANES, HEAD_DIM), F32),
            seq_scratch, seq_scratch, seq_scratch, seq_scratch, seq_scratch,
        ],
        compiler_params=pltpu.CompilerParams(
            dimension_semantics=("arbitrary", "arbitrary"), vmem_limit_bytes=VMEM_LIMIT),
        name="rglru_merge",
    )(zx, gb, ya, conv_w, conv_b, w_gates, b_gates, lam)


def _ffn_kernel(x_ref, m_ref, wo_ref, g2_ref, wi_ref, wf_ref, fg_ref, o_ref, ff_scr, *, final):
    x1 = x_ref[...] + jnp.dot(m_ref[...], wo_ref[...], preferred_element_type=F32)
    o_ref[...] = x1
    hb = (_rms_scale(x1) * g2_ref[...]).astype(BF16)
    for j in range(D_FF // FF_COLS):
        cols = slice(j * FF_COLS, (j + 1) * FF_COLS)
        gate = jnp.dot(hb, wi_ref[:, cols], preferred_element_type=F32)
        up = jnp.dot(hb, wi_ref[:, D_FF + j * FF_COLS:D_FF + (j + 1) * FF_COLS],
                     preferred_element_type=F32)
        ff_scr[:, cols] = (gate * _sigmoid(gate) * up).astype(BF16)
    x2 = o_ref[...] + jnp.dot(ff_scr[...], wf_ref[...], preferred_element_type=F32)
    if final:
        x2 = _rms_scale(x2) * fg_ref[...]
    o_ref[...] = x2


def _ffn(x, merged, layer, w_out, g2, w_ffn_in, w_ffn_out, final_g, final):
    t = x.shape[0]
    row_spec = pl.BlockSpec((ROW_TILE, D_MODEL), lambda i: (i, 0))
    return pl.pallas_call(
        functools.partial(_ffn_kernel, final=final),
        grid=(t // ROW_TILE,),
        in_specs=[
            row_spec, row_spec,
            _layer_spec(w_out.shape, layer),
            _const_spec((1, D_MODEL)),
            _layer_spec(w_ffn_in.shape, layer),
            _layer_spec(w_ffn_out.shape, layer),
            _const_spec((1, D_MODEL)),
        ],
        out_specs=row_spec,
        out_shape=jax.ShapeDtypeStruct((t, D_MODEL), F32),
        scratch_shapes=[pltpu.VMEM((ROW_TILE, D_FF), BF16)],
        compiler_params=pltpu.CompilerParams(
            dimension_semantics=("arbitrary",), vmem_limit_bytes=VMEM_LIMIT),
        name="out_proj_ffn",
    )(x, merged, w_out, g2, w_ffn_in, w_ffn_out, final_g)


def kernel(x, norm1_g, w_in, gmlp_ln_g, gmlp_ln_b, gmlp_w_s, gmlp_b_s, conv_w, conv_b,
           lru_w_r, lru_b_r, lru_w_i, lru_b_i, lru_lambda, w_out, norm2_g, w_ffn_in,
           w_ffn_out, final_g):
    bsz, seq, d = x.shape
    depth = w_in.shape[0]
    assert d == D_MODEL and ROW_TILE % CHUNK == 0 and seq % ROW_TILE == 0
    assert seq % (N_SEG * T_BLOCK) == 0 and (seq // N_SEG) % MERGE_ROWS == 0
    xf = x.reshape(bsz * seq, d)
    row = lambda v: v.reshape(1, -1)
    w_in = w_in * jnp.concatenate([jnp.ones((4 * d,), F32), jnp.full((2 * d,), 0.5, F32)])
    w_in, gmlp_w_s, w_out, w_ffn_in, w_ffn_out = (
        w.astype(BF16) for w in (w_in, gmlp_w_s, w_out, w_ffn_in, w_ffn_out))
    for l in range(depth):
        b_s = jnp.broadcast_to(gmlp_b_s[l][:, :, None], (GROUPS, CHUNK, GROUP_DIM))
        ya, zx, gb = _proj_in(xf, l, row(norm1_g[l]), w_in, row(gmlp_ln_g[l]),
                              row(gmlp_ln_b[l]), gmlp_w_s, b_s)
        w_gates = jnp.concatenate(
            [lru_w_r[l, 0], lru_w_i[l, 0], lru_w_r[l, 1], lru_w_i[l, 1]], axis=-1).astype(BF16)
        b_gates = 0.5 * jnp.stack(
            [lru_b_r[l, 0], lru_b_i[l, 0], lru_b_r[l, 1], lru_b_i[l, 1]], axis=0
        ).reshape(4, HEADS, HEAD_DIM).transpose(1, 0, 2).reshape(HEADS, 1, 4 * HEAD_DIM)
        shape3 = (bsz, seq, d)
        merged = _lru(zx.reshape(shape3), gb.reshape(shape3), ya.reshape(shape3),
                      0.5 * conv_w[l], row(0.5 * conv_b[l]), w_gates, b_gates, lru_lambda[l])
        xf = _ffn(xf, merged.reshape(bsz * seq, d), l, w_out, row(norm2_g[l]),
                  w_ffn_in, w_ffn_out, row(final_g), final=(l == depth - 1))
    return xf.reshape(bsz, seq, d)
```

```python
import functools

import jax
import jax.numpy as jnp
from jax import lax
from jax.experimental import pallas as pl
from jax.experimental.pallas import tpu as pltpu

D_MODEL = 1024
CHUNK = 128
GROUPS = 8
GROUP_DIM = D_MODEL // GROUPS
HEADS = 8
HEAD_DIM = D_MODEL // HEADS
CONV_WIDTH = 4
CONV_LEFT = 1
LRU_C = 8.0
N_DIRS = 2
D_FF = 2816
EPS = 1e-6

SUBLANES = 8
LANES = 128
ROW_TILE = 1024
PROJ_COLS = 256
PROJ_ROWS = 256
FF_COLS = 256
N_SEG = SUBLANES
T_BLOCK = 256
MERGE_ROWS = 256
LOG2_E = 1.4426950408889634
VMEM_LIMIT = 56 * 1024 * 1024

F32 = jnp.float32
BF16 = jnp.bfloat16


GELU_C0 = 0.7978845608028654
GELU_C1 = GELU_C0 * 0.044715


def _gelu_tanh(x):
    return jnp.tanh(x * (GELU_C1 * (x * x) + GELU_C0))


def _gelu(x):
    hx = 0.5 * x
    return hx * _gelu_tanh(x) + hx


def _gated_gelu(half_gate, x):
    qx = 0.25 * x
    return (qx * _gelu_tanh(x) + qx) * (jnp.tanh(half_gate) + 1.0)


def _sigmoid(x):
    return 1.0 / (1.0 + jnp.exp(-x))


def _rms_scale(x):
    return x * lax.rsqrt(jnp.mean(x * x, axis=-1, keepdims=True) + EPS)


def _const_spec(shape):
    return pl.BlockSpec(shape, lambda *_: (0,) * len(shape), pipeline_mode=pl.Buffered(1))


def _layer_spec(stacked_shape, layer):
    zeros = (0,) * (len(stacked_shape) - 1)
    return pl.BlockSpec((None,) + tuple(stacked_shape[1:]), lambda *_: (layer,) + zeros,
                        pipeline_mode=pl.Buffered(1))


def _proj_in_kernel(x_ref, g1_ref, w_ref, lng_ref, lnb_ref, ws_ref, bs_ref,
                    ya_ref, zx_ref, gb_ref, gv_scr, vn_scr):
    hb = (_rms_scale(x_ref[...]) * g1_ref[...]).astype(BF16)
    col_u, col_v, col_x, col_g, col_gate_a, col_gate_b = range(6)

    def proj(block, tile):
        rows, cols = tile
        c0 = block * D_MODEL + cols.start
        return jnp.dot(hb[rows], w_ref[:, c0:c0 + PROJ_COLS], preferred_element_type=F32)

    col_starts = range(0, D_MODEL, PROJ_COLS)
    row_blocks = [slice(r, r + PROJ_ROWS) for r in range(0, ROW_TILE, PROJ_ROWS)]
    tiles = [(rows, slice(c, c + PROJ_COLS)) for c in col_starts for rows in row_blocks]

    def layer_norm(rows):
        gv = gv_scr[rows, :]
        dv = gv - jnp.mean(gv, axis=-1, keepdims=True)
        vn = dv * lax.rsqrt(jnp.mean(dv * dv, axis=-1, keepdims=True) + EPS)
        vn_scr[rows, :] = (vn * lng_ref[...] + lnb_ref[...]).astype(BF16)

    for rows in row_blocks:
        for c in col_starts:
            tile = (rows, slice(c, c + PROJ_COLS))
            gv_scr[tile] = _gelu(proj(col_v, tile))
            zx_ref[tile] = proj(col_x, tile)
        layer_norm(rows)

    for tile in tiles:
        gb_ref[tile] = _gated_gelu(proj(col_gate_b, tile), proj(col_g, tile))

    def spatial_mix(g):
        cols = slice(g * GROUP_DIM, (g + 1) * GROUP_DIM)
        n_row_chunks = ROW_TILE // CHUNK
        v_g = jnp.concatenate(
            [vn_scr[c * CHUNK:(c + 1) * CHUNK, cols] for c in range(n_row_chunks)], axis=1)
        mixed = jnp.dot(ws_ref[g], v_g, preferred_element_type=F32)
        for c in range(n_row_chunks):
            ya_ref[c * CHUNK:(c + 1) * CHUNK, cols] = (
                mixed[:, c * GROUP_DIM:(c + 1) * GROUP_DIM] + bs_ref[g])

    for tile in tiles:
        rows, cols = tile
        if rows.start == 0:
            for g in range(cols.start // GROUP_DIM, cols.stop // GROUP_DIM):
                spatial_mix(g)
        ya_ref[tile] = ya_ref[tile] * _gated_gelu(proj(col_gate_a, tile), proj(col_u, tile))


def _proj_in(x, layer, g1, w_in, ln_g, ln_b, w_s, b_s):
    t = x.shape[0]
    row_spec = pl.BlockSpec((ROW_TILE, D_MODEL), lambda i: (i, 0))
    out = jax.ShapeDtypeStruct((t, D_MODEL), F32)
    return pl.pallas_call(
        _proj_in_kernel,
        grid=(t // ROW_TILE,),
        in_specs=[
            row_spec,
            _const_spec((1, D_MODEL)),
            _layer_spec(w_in.shape, layer),
            _const_spec((1, D_MODEL)),
            _const_spec((1, D_MODEL)),
            _layer_spec(w_s.shape, layer),
            _const_spec(b_s.shape),
        ],
        out_specs=[row_spec, row_spec, row_spec],
        out_shape=[out, out, out],
        scratch_shapes=[pltpu.VMEM((ROW_TILE, D_MODEL), F32),
                        pltpu.VMEM((ROW_TILE, D_MODEL), BF16)],
        compiler_params=pltpu.CompilerParams(
            dimension_semantics=("arbitrary",), vmem_limit_bytes=VMEM_LIMIT),
        name="proj_in",
    )(x, g1, w_in, ln_g, ln_b, w_s, b_s)


def _tile_scan(a, b, reverse):
    pos = lax.broadcasted_iota(jnp.int32, a.shape, 0)
    step = 1
    while step < N_SEG:
        if reverse:
            shift, valid = N_SEG - step, pos < N_SEG - step
        else:
            shift, valid = step, pos >= step
        a_nb = jnp.where(valid, pltpu.roll(a, shift, 0), 1.0)
        b_nb = jnp.where(valid, pltpu.roll(b, shift, 0), 0.0)
        b = a * b_nb + b
        a = a * a_nb
        step *= 2
    return b


def _shift_segments(v, up):
    pos = lax.broadcasted_iota(jnp.int32, v.shape, 0)
    if up:
        return jnp.where(pos < N_SEG - 1, pltpu.roll(v, N_SEG - 1, 0), 0.0)
    return jnp.where(pos >= 1, pltpu.roll(v, 1, 0), 0.0)


def _lru_kernel(zx_ref, gb_ref, ya_ref, cw_ref, cb_ref, wg_ref, bg_ref, lam_ref, out_ref,
                nat_scr, xh_scr, hf_scr, pf_scr, hb_scr, pb_scr):
    seq = zx_ref.shape[1]
    seg_len = seq // N_SEG
    pitch = seg_len + SUBLANES
    n_blocks = seg_len // T_BLOCK
    blk = T_BLOCK * N_SEG
    taps_after = CONV_WIDTH - 1 - CONV_LEFT

    halo = SUBLANES
    zero_rows = jnp.zeros((SUBLANES, LANES), F32)
    for s in range(N_SEG):
        first = s * pitch + halo
        nat_scr[first:first + seg_len, :] = zx_ref[0, s * seg_len:(s + 1) * seg_len, :]
        before = slice(first - CONV_LEFT, first)
        after = slice(first + seg_len, first + seg_len + taps_after)
        if s > 0:
            nat_scr[before, :] = zx_ref[0, s * seg_len - CONV_LEFT:s * seg_len, :]
        else:
            nat_scr[before, :] = zero_rows[:CONV_LEFT]
        if s < N_SEG - 1:
            nat_scr[after, :] = zx_ref[0, (s + 1) * seg_len:(s + 1) * seg_len + taps_after, :]
        else:
            nat_scr[after, :] = zero_rows[:taps_after]

    def conv_block(i, _):
        t0 = i * T_BLOCK
        steps = [nat_scr[pl.ds(t0 + halo - CONV_LEFT + j, N_SEG, stride=pitch), :]
                 for j in range(T_BLOCK + CONV_WIDTH - 1)]
        taps = [jnp.broadcast_to(cw_ref[k:k + 1, :], (N_SEG, LANES)) for k in range(CONV_WIDTH)]
        bias = jnp.broadcast_to(cb_ref[...], (N_SEG, LANES))
        tiles = []
        for tt in range(T_BLOCK):
            xh = bias
            for k in range(CONV_WIDTH):
                xh = xh + steps[tt + k] * taps[k]
            tiles.append(xh)
        xh_scr[pl.ds(pl.multiple_of(i * blk, blk), blk), :] = jnp.concatenate(tiles, axis=0)
        return 0

    lax.fori_loop(0, n_blocks, conv_block, 0)

    neg_lam = -lam_ref[...]
    softplus = jnp.maximum(neg_lam, 0.0) + jnp.log1p(jnp.exp(-jnp.abs(neg_lam)))
    half_rate = (0.5 * LRU_C) * softplus

    def decay_and_input(xh, d):
        cols = slice(d * 2 * HEAD_DIM, (d + 1) * 2 * HEAD_DIM)
        g = jnp.dot(xh.astype(BF16), wg_ref[0, :, cols], preferred_element_type=F32)
        g = jnp.tanh(g + bg_ref[0, :, cols])
        t_r, t_i = g[:, :HEAD_DIM], g[:, HEAD_DIM:]
        neg_log_a = half_rate[d:d + 1] * t_r + half_rate[d:d + 1]
        a = jnp.exp2(neg_log_a * (-LOG2_E))
        m2 = jnp.maximum(jnp.tanh(neg_log_a) * (a * a + 1.0), 0.0)
        mult = jnp.where(m2 > 0.0, m2 * lax.rsqrt(m2), 0.0)
        return a, mult * (t_i * xh + xh)

    def local_scan(a, b, prod, h, reverse):
        order = range(T_BLOCK - 1, -1, -1) if reverse else range(T_BLOCK)
        hs, ps = [None] * T_BLOCK, [None] * T_BLOCK
        for tt in order:
            rows = slice(tt * N_SEG, (tt + 1) * N_SEG)
            h = a[rows] * h + b[rows]
            prod = prod * a[rows]
            hs[tt], ps[tt] = h, prod
        return jnp.concatenate(hs, axis=0), jnp.concatenate(ps, axis=0), prod, h

    def scan_block(i, carry):
        prod_f, h_f, prod_b, h_b = carry
        rows_f = pl.ds(pl.multiple_of(i * blk, blk), blk)
        rows_b = pl.ds(pl.multiple_of((n_blocks - 1 - i) * blk, blk), blk)
        a, b = decay_and_input(xh_scr[rows_f, :], 0)
        hf_scr[rows_f, :], pf_scr[rows_f, :], prod_f, h_f = local_scan(a, b, prod_f, h_f, False)
        a, b = decay_and_input(xh_scr[rows_b, :], 1)
        hb_scr[rows_b, :], pb_scr[rows_b, :], prod_b, h_b = local_scan(a, b, prod_b, h_b, True)
        return prod_f, h_f, prod_b, h_b

    one = jnp.ones((N_SEG, LANES), F32)
    zero = jnp.zeros((N_SEG, LANES), F32)
    prod_f, h_f, prod_b, h_b = lax.fori_loop(0, n_blocks, scan_block, (one, zero, one, zero))

    carry_f = _shift_segments(_tile_scan(prod_f, h_f, reverse=False), up=False)
    carry_b = _shift_segments(_tile_scan(prod_b, h_b, reverse=True), up=True)

    def fix_block(i, _):
        rows = pl.ds(pl.multiple_of(i * blk, blk), blk)
        shape = (T_BLOCK, N_SEG, LANES)
        h = (hf_scr[rows, :].reshape(shape) + pf_scr[rows, :].reshape(shape) * carry_f
             + hb_scr[rows, :].reshape(shape) + pb_scr[rows, :].reshape(shape) * carry_b)
        hf_scr[rows, :] = h.reshape(blk, LANES)
        return 0

    lax.fori_loop(0, n_blocks, fix_block, 0)

    for s in range(N_SEG):
        for t0 in range(0, seg_len, MERGE_ROWS):
            tiles = [hf_scr[pl.ds((t0 + q * SUBLANES) * N_SEG + s, SUBLANES, stride=N_SEG), :]
                     for q in range(MERGE_ROWS // SUBLANES)]
            h = jnp.concatenate(tiles, axis=0)
            rows = slice(s * seg_len + t0, s * seg_len + t0 + MERGE_ROWS)
            merged = ya_ref[0, rows, :] + gb_ref[0, rows, :] * h
            out_ref[0, rows, :] = merged.astype(out_ref.dtype)


def _lru(zx, gb, ya, conv_w, conv_b, w_gates, b_gates, lam):
    bsz, seq, _ = zx.shape
    seg_len = seq // N_SEG
    slab = pl.BlockSpec((1, seq, HEAD_DIM), lambda b, h: (b, 0, h))
    per_head = lambda rows: pl.BlockSpec((rows, HEAD_DIM), lambda b, h: (0, h))
    seq_scratch = pltpu.VMEM((seq, HEAD_DIM), F32)
    return pl.pallas_call(
        _lru_kernel,
        grid=(bsz, HEADS),
        in_specs=[
            slab, slab, slab,
            per_head(CONV_WIDTH),
            per_head(1),
            pl.BlockSpec((1, HEAD_DIM, 4 * HEAD_DIM), lambda b, h: (h, 0, 0)),
            pl.BlockSpec((1, 1, 4 * HEAD_DIM), lambda b, h: (h, 0, 0)),
            per_head(N_DIRS),
        ],
        out_specs=slab,
        out_shape=jax.ShapeDtypeStruct(zx.shape, BF16),
        scratch_shapes=[
            pltpu.VMEM((N_SEG * (seg_len + SUBLANES) + SUBLANES, HEAD_DIM), F32),
            seq_scratch, seq_scratch, seq_scratch, seq_scratch, seq_scratch,
        ],
        compiler_params=pltpu.CompilerParams(
            dimension_semantics=("arbitrary", "arbitrary"), vmem_limit_bytes=VMEM_LIMIT),
        name="rglru_merge",
    )(zx, gb, ya, conv_w, conv_b, w_gates, b_gates, lam)


def _ffn_kernel(x_ref, m_ref, wo_ref, g2_ref, wi_ref, wf_ref, fg_ref, o_ref, ff_scr, *, final):
    x1 = x_ref[...] + jnp.dot(m_ref[...], wo_ref[...], preferred_element_type=F32)
    o_ref[...] = x1
    hb = (_rms_scale(x1) * g2_ref[...]).astype(BF16)
    for j in range(D_FF // FF_COLS):
        cols = slice(j * FF_COLS, (j + 1) * FF_COLS)
        gate = jnp.dot(hb, wi_ref[:, cols], preferred_element_type=F32)
        up = jnp.dot(hb, wi_ref[:, D_FF + j * FF_COLS:D_FF + (j + 1) * FF_COLS],
                     preferred_element_type=F32)
        ff_scr[:, cols] = (gate * _sigmoid(gate) * up).astype(BF16)
    x2 = o_ref[...] + jnp.dot(ff_scr[...], wf_ref[...], preferred_element_type=F32)
    if final:
        x2 = _rms_scale(x2) * fg_ref[...]
    o_ref[...] = x2


def _ffn(x, merged, layer, w_out, g2, w_ffn_in, w_ffn_out, final_g, final):
    t = x.shape[0]
    row_spec = pl.BlockSpec((ROW_TILE, D_MODEL), lambda i: (i, 0))
    return pl.pallas_call(
        functools.partial(_ffn_kernel, final=final),
        grid=(t // ROW_TILE,),
        in_specs=[
            row_spec, row_spec,
            _layer_spec(w_out.shape, layer),
            _const_spec((1, D_MODEL)),
            _layer_spec(w_ffn_in.shape, layer),
            _layer_spec(w_ffn_out.shape, layer),
            _const_spec((1, D_MODEL)),
        ],
        out_specs=row_spec,
        out_shape=jax.ShapeDtypeStruct((t, D_MODEL), F32),
        scratch_shapes=[pltpu.VMEM((ROW_TILE, D_FF), BF16)],
        compiler_params=pltpu.CompilerParams(
            dimension_semantics=("arbitrary",), vmem_limit_bytes=VMEM_LIMIT),
        name="out_proj_ffn",
    )(x, merged, w_out, g2, w_ffn_in, w_ffn_out, final_g)


def kernel(x, norm1_g, w_in, gmlp_ln_g, gmlp_ln_b, gmlp_w_s, gmlp_b_s, conv_w, conv_b,
           lru_w_r, lru_b_r, lru_w_i, lru_b_i, lru_lambda, w_out, norm2_g, w_ffn_in,
           w_ffn_out, final_g):
    bsz, seq, d = x.shape
    depth = w_in.shape[0]
    assert d == D_MODEL and ROW_TILE % CHUNK == 0 and seq % ROW_TILE == 0
    assert seq % (N_SEG * T_BLOCK) == 0 and (seq // N_SEG) % MERGE_ROWS == 0
    xf = x.reshape(bsz * seq, d)
    row = lambda v: v.reshape(1, -1)
    w_in = w_in * jnp.concatenate([jnp.ones((4 * d,), F32), jnp.full((2 * d,), 0.5, F32)])
    w_in, gmlp_w_s, w_out, w_ffn_in, w_ffn_out = (
        w.astype(BF16) for w in (w_in, gmlp_w_s, w_out, w_ffn_in, w_ffn_out))
    for l in range(depth):
        b_s = jnp.broadcast_to(gmlp_b_s[l][:, :, None], (GROUPS, CHUNK, GROUP_DIM))
        ya, zx, gb = _proj_in(xf, l, row(norm1_g[l]), w_in, row(gmlp_ln_g[l]),
                              row(gmlp_ln_b[l]), gmlp_w_s, b_s)
        w_gates = jnp.concatenate(
            [lru_w_r[l, 0], lru_w_i[l, 0], lru_w_r[l, 1], lru_w_i[l, 1]], axis=-1).astype(BF16)
        b_gates = 0.5 * jnp.stack(
            [lru_b_r[l, 0], lru_b_i[l, 0], lru_b_r[l, 1], lru_b_i[l, 1]], axis=0
        ).reshape(4, HEADS, HEAD_DIM).transpose(1, 0, 2).reshape(HEADS, 1, 4 * HEAD_DIM)
        shape3 = (bsz, seq, d)
        merged = _lru(zx.reshape(shape3), gb.reshape(shape3), ya.reshape(shape3),
                      0.5 * conv_w[l], row(0.5 * conv_b[l]), w_gates, b_gates, lru_lambda[l])
        xf = _ffn(xf, merged.reshape(bsz * seq, d), l, w_out, row(norm2_g[l]),
                  w_ffn_in, w_ffn_out, row(final_g), final=(l == depth - 1))
    return xf.reshape(bsz, seq, d)
```

```python
import functools

import jax
import jax.numpy as jnp
from jax import lax
from jax.experimental import pallas as pl
from jax.experimental.pallas import tpu as pltpu

D_MODEL = 1024
CHUNK = 128
GROUPS = 8
GROUP_DIM = D_MODEL // GROUPS
HEADS = 8
HEAD_DIM = D_MODEL // HEADS
CONV_WIDTH = 4
CONV_LEFT = 1
LRU_C = 8.0
N_DIRS = 2
D_FF = 2816
EPS = 1e-6

SUBLANES = 8
LANES = 128
ROW_TILE = 1024
PROJ_COLS = 256
PROJ_ROWS = 256
FF_COLS = 256
N_SEG = SUBLANES
T_BLOCK = 256
MERGE_ROWS = 256
LOG2_E = 1.4426950408889634
VMEM_LIMIT = 56 * 1024 * 1024

F32 = jnp.float32
BF16 = jnp.bfloat16


GELU_C0 = 0.7978845608028654
GELU_C1 = GELU_C0 * 0.044715


def _gelu_tanh(x):
    return jnp.tanh(x * (GELU_C1 * (x * x) + GELU_C0))


def _gelu(x):
    hx = 0.5 * x
    return hx * _gelu_tanh(x) + hx


def _gated_gelu(half_gate, x):
    qx = 0.25 * x
    return (qx * _gelu_tanh(x) + qx) * (jnp.tanh(half_gate) + 1.0)


def _sigmoid(x):
    return 1.0 / (1.0 + jnp.exp(-x))


def _rms_scale(x):
    return x * lax.rsqrt(jnp.mean(x * x, axis=-1, keepdims=True) + EPS)


def _const_spec(shape):
    return pl.BlockSpec(shape, lambda *_: (0,) * len(shape), pipeline_mode=pl.Buffered(1))


def _layer_spec(stacked_shape, layer):
    zeros = (0,) * (len(stacked_shape) - 1)
    return pl.BlockSpec((None,) + tuple(stacked_shape[1:]), lambda *_: (layer,) + zeros,
                        pipeline_mode=pl.Buffered(1))


def _proj_in_kernel(x_ref, g1_ref, w_ref, lng_ref, lnb_ref, ws_ref, bs_ref,
                    ya_ref, zx_ref, gb_ref, gv_scr, vn_scr):
    hb = (_rms_scale(x_ref[...]) * g1_ref[...]).astype(BF16)
    col_u, col_v, col_x, col_g, col_gate_a, col_gate_b = range(6)

    def proj(block, tile):
        rows, cols = tile
        c0 = block * D_MODEL + cols.start
        return jnp.dot(hb[rows], w_ref[:, c0:c0 + PROJ_COLS], preferred_element_type=F32)

    col_starts = range(0, D_MODEL, PROJ_COLS)
    row_blocks = [slice(r, r + PROJ_ROWS) for r in range(0, ROW_TILE, PROJ_ROWS)]
    tiles = [(rows, slice(c, c + PROJ_COLS)) for c in col_starts for rows in row_blocks]

    def layer_norm(rows):
        gv = gv_scr[rows, :]
        dv = gv - jnp.mean(gv, axis=-1, keepdims=True)
        vn = dv * lax.rsqrt(jnp.mean(dv * dv, axis=-1, keepdims=True) + EPS)
        vn_scr[rows, :] = (vn * lng_ref[...] + lnb_ref[...]).astype(BF16)

    for rows in row_blocks:
        for c in col_starts:
            tile = (rows, slice(c, c + PROJ_COLS))
            gv_scr[tile] = _gelu(proj(col_v, tile))
            zx_ref[tile] = proj(col_x, tile)
        layer_norm(rows)

    for tile in tiles:
        gb_ref[tile] = _gated_gelu(proj(col_gate_b, tile), proj(col_g, tile))

    def spatial_mix(g):
        cols = slice(g * GROUP_DIM, (g + 1) * GROUP_DIM)
        n_row_chunks = ROW_TILE // CHUNK
        v_g = jnp.concatenate(
            [vn_scr[c * CHUNK:(c + 1) * CHUNK, cols] for c in range(n_row_chunks)], axis=1)
        mixed = jnp.dot(ws_ref[g], v_g, preferred_element_type=F32)
        for c in range(n_row_chunks):
            ya_ref[c * CHUNK:(c + 1) * CHUNK, cols] = (
                mixed[:, c * GROUP_DIM:(c + 1) * GROUP_DIM] + bs_ref[g])

    for tile in tiles:
        rows, cols = tile
        if rows.start == 0:
            for g in range(cols.start // GROUP_DIM, cols.stop // GROUP_DIM):
                spatial_mix(g)
        ya_ref[tile] = ya_ref[tile] * _gated_gelu(proj(col_gate_a, tile), proj(col_u, tile))


def _proj_in(x, layer, g1, w_in, ln_g, ln_b, w_s, b_s):
    t = x.shape[0]
    row_spec = pl.BlockSpec((ROW_TILE, D_MODEL), lambda i: (i, 0))
    out = jax.ShapeDtypeStruct((t, D_MODEL), F32)
    return pl.pallas_call(
        _proj_in_kernel,
        grid=(t // ROW_TILE,),
        in_specs=[
            row_spec,
            _const_spec((1, D_MODEL)),
            _layer_spec(w_in.shape, layer),
            _const_spec((1, D_MODEL)),
            _const_spec((1, D_MODEL)),
            _layer_spec(w_s.shape, layer),
            _const_spec(b_s.shape),
        ],
        out_specs=[row_spec, row_spec, row_spec],
        out_shape=[out, out, out],
        scratch_shapes=[pltpu.VMEM((ROW_TILE, D_MODEL), F32),
                        pltpu.VMEM((ROW_TILE, D_MODEL), BF16)],
        compiler_params=pltpu.CompilerParams(
            dimension_semantics=("arbitrary",), vmem_limit_bytes=VMEM_LIMIT),
        name="proj_in",
    )(x, g1, w_in, ln_g, ln_b, w_s, b_s)


def _tile_scan(a, b, reverse):
    pos = lax.broadcasted_iota(jnp.int32, a.shape, 0)
    step = 1
    while step < N_SEG:
        if reverse:
            shift, valid = N_SEG - step, pos < N_SEG - step
        else:
            shift, valid = step, pos >= step
        a_nb = jnp.where(valid, pltpu.roll(a, shift, 0), 1.0)
        b_nb = jnp.where(valid, pltpu.roll(b, shift, 0), 0.0)
        b = a * b_nb + b
        a = a * a_nb
        step *= 2
    return b


def _shift_segments(v, up):
    pos = lax.broadcasted_iota(jnp.int32, v.shape, 0)
    if up:
        return jnp.where(pos < N_SEG - 1, pltpu.roll(v, N_SEG - 1, 0), 0.0)
    return jnp.where(pos >= 1, pltpu.roll(v, 1, 0), 0.0)


def _lru_kernel(zx_ref, gb_ref, ya_ref, cw_ref, cb_ref, wg_ref, bg_ref, lam_ref, out_ref,
                nat_scr, xh_scr, hf_scr, pf_scr, hb_scr, pb_scr):
    seq = zx_ref.shape[1]
    seg_len = seq // N_SEG
    pitch = seg_len + SUBLANES
    n_blocks = seg_len // T_BLOCK
    blk = T_BLOCK * N_SEG
    taps_after = CONV_WIDTH - 1 - CONV_LEFT

    halo = SUBLANES
    zero_rows = jnp.zeros((SUBLANES, LANES), F32)
    for s in range(N_SEG):
        first = s * pitch + halo
        nat_scr[first:first + seg_len, :] = zx_ref[0, s * seg_len:(s + 1) * seg_len, :]
        before = slice(first - CONV_LEFT, first)
        after = slice(first + seg_len, first + seg_len + taps_after)
        if s > 0:
            nat_scr[before, :] = zx_ref[0, s * seg_len - CONV_LEFT:s * seg_len, :]
        else:
            nat_scr[before, :] = zero_rows[:CONV_LEFT]
        if s < N_SEG - 1:
            nat_scr[after, :] = zx_ref[0, (s + 1) * seg_len:(s + 1) * seg_len + taps_after, :]
        else:
            nat_scr[after, :] = zero_rows[:taps_after]

    def conv_block(i, _):
        t0 = i * T_BLOCK
        steps = [nat_scr[pl.ds(t0 + halo - CONV_LEFT + j, N_SEG, stride=pitch), :]
                 for j in range(T_BLOCK + CONV_WIDTH - 1)]
        taps = [jnp.broadcast_to(cw_ref[k:k + 1, :], (N_SEG, LANES)) for k in range(CONV_WIDTH)]
        bias = jnp.broadcast_to(cb_ref[...], (N_SEG, LANES))
        tiles = []
        for tt in range(T_BLOCK):
            xh = bias
            for k in range(CONV_WIDTH):
                xh = xh + steps[tt + k] * taps[k]
            tiles.append(xh)
        xh_scr[pl.ds(pl.multiple_of(i * blk, blk), blk), :] = jnp.concatenate(tiles, axis=0)
        return 0

    lax.fori_loop(0, n_blocks, conv_block, 0)

    neg_lam = -lam_ref[...]
    softplus = jnp.maximum(neg_lam, 0.0) + jnp.log1p(jnp.exp(-jnp.abs(neg_lam)))
    half_rate = (0.5 * LRU_C) * softplus

    def decay_and_input(xh, d):
        cols = slice(d * 2 * HEAD_DIM, (d + 1) * 2 * HEAD_DIM)
        g = jnp.dot(xh.astype(BF16), wg_ref[0, :, cols], preferred_element_type=F32)
        g = jnp.tanh(g + bg_ref[0, :, cols])
        t_r, t_i = g[:, :HEAD_DIM], g[:, HEAD_DIM:]
        neg_log_a = half_rate[d:d + 1] * t_r + half_rate[d:d + 1]
        a = jnp.exp2(neg_log_a * (-LOG2_E))
        m2 = jnp.maximum(jnp.tanh(neg_log_a) * (a * a + 1.0), 0.0)
        mult = jnp.where(m2 > 0.0, m2 * lax.rsqrt(m2), 0.0)
        return a, mult * (t_i * xh + xh)

    def local_scan(a, b, prod, h, reverse):
        order = range(T_BLOCK - 1, -1, -1) if reverse else range(T_BLOCK)
        hs, ps = [None] * T_BLOCK, [None] * T_BLOCK
        for tt in order:
            rows = slice(tt * N_SEG, (tt + 1) * N_SEG)
            h = a[rows] * h + b[rows]
            prod = prod * a[rows]
            hs[tt], ps[tt] = h, prod
        return jnp.concatenate(hs, axis=0), jnp.concatenate(ps, axis=0), prod, h

    def scan_block(i, carry):
        prod_f, h_f, prod_b, h_b = carry
        rows_f = pl.ds(pl.multiple_of(i * blk, blk), blk)
        rows_b = pl.ds(pl.multiple_of((n_blocks - 1 - i) * blk, blk), blk)
        a, b = decay_and_input(xh_scr[rows_f, :], 0)
        hf_scr[rows_f, :], pf_scr[rows_f, :], prod_f, h_f = local_scan(a, b, prod_f, h_f, False)
        a, b = decay_and_input(xh_scr[rows_b, :], 1)
        hb_scr[rows_b, :], pb_scr[rows_b, :], prod_b, h_b = local_scan(a, b, prod_b, h_b, True)
        return prod_f, h_f, prod_b, h_b

    one = jnp.ones((N_SEG, LANES), F32)
    zero = jnp.zeros((N_SEG, LANES), F32)
    prod_f, h_f, prod_b, h_b = lax.fori_loop(0, n_blocks, scan_block, (one, zero, one, zero))

    carry_f = _shift_segments(_tile_scan(prod_f, h_f, reverse=False), up=False)
    carry_b = _shift_segments(_tile_scan(prod_b, h_b, reverse=True), up=True)

    def fix_block(i, _):
        rows = pl.ds(pl.multiple_of(i * blk, blk), blk)
        shape = (T_BLOCK, N_SEG, LANES)
        h = (hf_scr[rows, :].reshape(shape) + pf_scr[rows, :].reshape(shape) * carry_f
             + hb_scr[rows, :].reshape(shape) + pb_scr[rows, :].reshape(shape) * carry_b)
        for tt in range(T_BLOCK):
            nat_scr[pl.ds(i * T_BLOCK + halo + tt, N_SEG, stride=pitch), :] = h[tt]
        return 0

    lax.fori_loop(0, n_blocks, fix_block, 0)

    for s in range(N_SEG):
        for t0 in range(0, seg_len, MERGE_ROWS):
            first = s * pitch + halo + t0
            rows = slice(s * seg_len + t0, s * seg_len + t0 + MERGE_ROWS)
            merged = ya_ref[0, rows, :] + gb_ref[0, rows, :] * nat_scr[first:first + MERGE_ROWS, :]
            out_ref[0, rows, :] = merged.astype(out_ref.dtype)


def _lru(zx, gb, ya, conv_w, conv_b, w_gates, b_gates, lam):
    bsz, seq, _ = zx.shape
    seg_len = seq // N_SEG
    slab = pl.BlockSpec((1, seq, HEAD_DIM), lambda b, h: (b, 0, h))
    per_head = lambda rows: pl.BlockSpec((rows, HEAD_DIM), lambda b, h: (0, h))
    seq_scratch = pltpu.VMEM((seq, HEAD_DIM), F32)
    return pl.pallas_call(
        _lru_kernel,
        grid=(bsz, HEADS),
        in_specs=[
            slab, slab, slab,
            per_head(CONV_WIDTH),
            per_head(1),
            pl.BlockSpec((1, HEAD_DIM, 4 * HEAD_DIM), lambda b, h: (h, 0, 0)),
            pl.BlockSpec((1, 1, 4 * HEAD_DIM), lambda b, h: (h, 0, 0)),
            per_head(N_DIRS),
        ],
        out_specs=slab,
        out_shape=jax.ShapeDtypeStruct(zx.shape, BF16),
        scratch_shapes=[
            pltpu.VMEM((N_SEG * (seg_len + SUBLANES) + SUBLANES, HEAD_DIM), F32),
            seq_scratch, seq_scratch, seq_scratch, seq_scratch, seq_scratch,
        ],
        compiler_params=pltpu.CompilerParams(
            dimension_semantics=("arbitrary", "arbitrary"), vmem_limit_bytes=VMEM_LIMIT),
        name="rglru_merge",
    )(zx, gb, ya, conv_w, conv_b, w_gates, b_gates, lam)


def _ffn_kernel(x_ref, m_ref, wo_ref, g2_ref, wi_ref, wf_ref, fg_ref, o_ref, ff_scr, *, final):
    x1 = x_ref[...] + jnp.dot(m_ref[...], wo_ref[...], preferred_element_type=F32)
    o_ref[...] = x1
    hb = (_rms_scale(x1) * g2_ref[...]).astype(BF16)
    for j in range(D_FF // FF_COLS):
        cols = slice(j * FF_COLS, (j + 1) * FF_COLS)
        gate = jnp.dot(hb, wi_ref[:, cols], preferred_element_type=F32)
        up = jnp.dot(hb, wi_ref[:, D_FF + j * FF_COLS:D_FF + (j + 1) * FF_COLS],
                     preferred_element_type=F32)
        ff_scr[:, cols] = (gate * _sigmoid(gate) * up).astype(BF16)
    x2 = o_ref[...] + jnp.dot(ff_scr[...], wf_ref[...], preferred_element_type=F32)
    if final:
        x2 = _rms_scale(x2) * fg_ref[...]
    o_ref[...] = x2


def _ffn(x, merged, layer, w_out, g2, w_ffn_in, w_ffn_out, final_g, final):
    t = x.shape[0]
    row_spec = pl.BlockSpec((ROW_TILE, D_MODEL), lambda i: (i, 0))
    return pl.pallas_call(
        functools.partial(_ffn_kernel, final=final),
        grid=(t // ROW_TILE,),
        in_specs=[
            row_spec, row_spec,
            _layer_spec(w_out.shape, layer),
            _const_spec((1, D_MODEL)),
            _layer_spec(w_ffn_in.shape, layer),
            _layer_spec(w_ffn_out.shape, layer),
            _const_spec((1, D_MODEL)),
        ],
        out_specs=row_spec,
        out_shape=jax.ShapeDtypeStruct((t, D_MODEL), F32),
        scratch_shapes=[pltpu.VMEM((ROW_TILE, D_FF), BF16)],
        compiler_params=pltpu.CompilerParams(
            dimension_semantics=("arbitrary",), vmem_limit_bytes=VMEM_LIMIT),
        name="out_proj_ffn",
    )(x, merged, w_out, g2, w_ffn_in, w_ffn_out, final_g)


def kernel(x, norm1_g, w_in, gmlp_ln_g, gmlp_ln_b, gmlp_w_s, gmlp_b_s, conv_w, conv_b,
           lru_w_r, lru_b_r, lru_w_i, lru_b_i, lru_lambda, w_out, norm2_g, w_ffn_in,
           w_ffn_out, final_g):
    bsz, seq, d = x.shape
    depth = w_in.shape[0]
    assert d == D_MODEL and ROW_TILE % CHUNK == 0 and seq % ROW_TILE == 0
    assert seq % (N_SEG * T_BLOCK) == 0 and (seq // N_SEG) % MERGE_ROWS == 0
    xf = x.reshape(bsz * seq, d)
    row = lambda v: v.reshape(1, -1)
    w_in = w_in * jnp.concatenate([jnp.ones((4 * d,), F32), jnp.full((2 * d,), 0.5, F32)])
    w_in, gmlp_w_s, w_out, w_ffn_in, w_ffn_out = (
        w.astype(BF16) for w in (w_in, gmlp_w_s, w_out, w_ffn_in, w_ffn_out))
    for l in range(depth):
        b_s = jnp.broadcast_to(gmlp_b_s[l][:, :, None], (GROUPS, CHUNK, GROUP_DIM))
        ya, zx, gb = _proj_in(xf, l, row(norm1_g[l]), w_in, row(gmlp_ln_g[l]),
                              row(gmlp_ln_b[l]), gmlp_w_s, b_s)
        w_gates = jnp.concatenate(
            [lru_w_r[l, 0], lru_w_i[l, 0], lru_w_r[l, 1], lru_w_i[l, 1]], axis=-1).astype(BF16)
        b_gates = 0.5 * jnp.stack(
            [lru_b_r[l, 0], lru_b_i[l, 0], lru_b_r[l, 1], lru_b_i[l, 1]], axis=0
        ).reshape(4, HEADS, HEAD_DIM).transpose(1, 0, 2).reshape(HEADS, 1, 4 * HEAD_DIM)
        shape3 = (bsz, seq, d)
        merged = _lru(zx.reshape(shape3), gb.reshape(shape3), ya.reshape(shape3),
                      0.5 * conv_w[l], row(0.5 * conv_b[l]), w_gates, b_gates, lru_lambda[l])
        xf = _ffn(xf, merged.reshape(bsz * seq, d), l, w_out, row(norm2_g[l]),
                  w_ffn_in, w_ffn_out, row(final_g), final=(l == depth - 1))
    return xf.reshape(bsz, seq, d)
```
